```python
import math
import jax
import jax.numpy as jnp
from jax import lax
import numpy as np

D_MODEL = 1024
BATCH = 32
SEQ = 2048
DEPTH = 4
DEC_BATCH = 16
DEC_SEQ = 64
PAST_LEN = 4096

CHUNK = 64
N_MIXERS = 2
N_A = (DEPTH + 1) // 2
N_B = DEPTH // 2
SGU_CHUNK = 128
D_SGU = 2 * D_MODEL
SGU_GROUP_DIM = 128
SGU_GROUPS = D_SGU // SGU_GROUP_DIM
N_HEADS = 16
Q_LORA = 384
KV_LORA = 256
NOPE_DIM = 64
ROPE_DIM = 32
V_DIM = 64
QK_DIM = NOPE_DIM + ROPE_DIM
ATTN_SCALE = 1.0 / math.sqrt(QK_DIM)
ROPE_BASE = 10000.0
Q_BLOCK = 128
D_FF = 4 * D_MODEL
EPS = 1e-6
NEG_INF = -1e30

kernel_name = 'streaming_sgu_mla_hybrid_step'


def rms_norm(x, g):
    xf = x.astype(jnp.float32)
    y = xf * lax.rsqrt(jnp.mean(xf * xf, axis=-1, keepdims=True) + EPS)
    return (y * g.astype(jnp.float32)).astype(x.dtype)


def rope(x, pos):
    half = ROPE_DIM // 2
    inv = ROPE_BASE ** (-jnp.arange(half, dtype=jnp.float32) / half)
    ang = pos.astype(jnp.float32)[:, None] * inv[None, :]
    ang = ang.reshape((ang.shape[0],) + (1,) * (x.ndim - 3) + (half,))
    cos, sin = jnp.cos(ang), jnp.sin(ang)
    xf = x.astype(jnp.float32)
    x1, x2 = xf[..., :half], xf[..., half:]
    return jnp.concatenate([x1 * cos - x2 * sin, x1 * sin + x2 * cos], axis=-1).astype(x.dtype)


def chunk_causal_mask(n):
    c = jnp.arange(n) // CHUNK
    return c[:, None] >= c[None, :]


def sgu_mixer(h, w_in, norm_g, w_s, b_s, w_out, rows):
    B, S, _ = h.shape
    z = jax.nn.gelu(h @ w_in, approximate=False)
    u, v = jnp.split(z, 2, axis=-1)
    v = rms_norm(v, norm_g)
    ws = w_s[:, :rows, :rows] * chunk_causal_mask(rows).astype(w_s.dtype)[None]
    vg = v.reshape(B, S // rows, rows, SGU_GROUPS, SGU_GROUP_DIM)
    mixed = jnp.einsum('gij,bcjgd->bcigd', ws, vg) + b_s[:, :rows].T[None, None, :, :, None]
    y = u * mixed.reshape(B, S, D_SGU)
    return y @ w_out, v


def mla_project(h, pos, w_in, q_norm_g, kv_norm_g, w_uq, q_nope_g, q_rope_g, k_rope_g):
    B, S, _ = h.shape
    a = h @ w_in
    c_q, c_kv, k_r = jnp.split(a, [Q_LORA, Q_LORA + KV_LORA], axis=-1)
    c_q = rms_norm(c_q, q_norm_g)
    c_kv = rms_norm(c_kv, kv_norm_g)
    k_r = rope(rms_norm(k_r, k_rope_g), pos)
    q = (c_q @ w_uq).reshape(B, S, N_HEADS, QK_DIM)
    q_nope = rms_norm(q[..., :NOPE_DIM], q_nope_g)
    q_rope = rope(rms_norm(q[..., NOPE_DIM:], q_rope_g), pos)
    return q_nope, q_rope, c_kv, k_r


def mla_expand(c_kv, w_ukv, k_nope_g):
    B, S, _ = c_kv.shape
    kv = (c_kv @ w_ukv).reshape(B, S, N_HEADS, NOPE_DIM + V_DIM)
    k_nope = rms_norm(kv[..., :NOPE_DIM], k_nope_g)
    return k_nope, kv[..., NOPE_DIM:]


def mla_attend(q_nope, q_rope, k_nope, k_rope, v, mask):
    s = (jnp.einsum('bqhd,bkhd->bhqk', q_nope, k_nope, preferred_element_type=jnp.float32)
         + jnp.einsum('bqhr,bkr->bhqk', q_rope, k_rope, preferred_element_type=jnp.float32)) * ATTN_SCALE
    if mask is not None:
        s = jnp.where(mask, s, NEG_INF)
    p = jax.nn.softmax(s, axis=-1).astype(v.dtype)
    return jnp.einsum('bhqk,bkhd->bqhd', p, v)


def mla_prompt_attention(q_nope, q_rope, k_nope, k_rope, v):
    B, S = q_nope.shape[:2]
    nq = S // Q_BLOCK
    qn_b = q_nope.reshape(B, nq, Q_BLOCK, N_HEADS, NOPE_DIM).transpose(1, 0, 2, 3, 4)
    qr_b = q_rope.reshape(B, nq, Q_BLOCK, N_HEADS, ROPE_DIM).transpose(1, 0, 2, 3, 4)
    k_chunk = jnp.arange(S) // CHUNK

    def one_block(args):
        bi, qn, qr = args
        q_chunk = (bi * Q_BLOCK + jnp.arange(Q_BLOCK)) // CHUNK
        mask = k_chunk[None, :] <= q_chunk[:, None]
        return mla_attend(qn, qr, k_nope, k_rope, v, mask[None, None])

    o = lax.map(one_block, (jnp.arange(nq), qn_b, qr_b))
    return o.transpose(1, 0, 2, 3, 4).reshape(B, S, N_HEADS * V_DIM)


def sqrelu_mlp(h, w_up, w_down):
    return jnp.square(jax.nn.relu(h @ w_up)) @ w_down


def setup_inputs(seed: int = 0) -> dict:
    key = jax.random.key(seed)
    ks = jax.random.split(key, 24)

    def nrm(k, shape, scale):
        return jax.random.normal(k, shape, jnp.float32) * scale

    def gain(k, shape):
        return 1.0 + 0.1 * jax.random.normal(k, shape, jnp.float32)

    return {
        'x_prompt': nrm(ks[0], (BATCH, SEQ, D_MODEL), 1.0),
        'x_sample': nrm(ks[1], (DEC_BATCH, DEC_SEQ, D_MODEL), 1.0),
        'cache_kv_latent': nrm(ks[2], (N_B, DEC_BATCH, PAST_LEN, KV_LORA), 1.0),
        'cache_k_rope': nrm(ks[3], (N_B, DEC_BATCH, PAST_LEN, ROPE_DIM), 1.0),
        'mix_norm_g': gain(ks[4], (DEPTH, D_MODEL)),
        'mlp_norm_g': gain(ks[5], (DEPTH, D_MODEL)),
        'sgu_w_in': nrm(ks[6], (N_A, D_MODEL, 2 * D_SGU), D_MODEL ** -0.5),
        'sgu_norm_g': gain(ks[7], (N_A, D_SGU)),
        'sgu_w_s': nrm(ks[8], (N_A, SGU_GROUPS, SGU_CHUNK, SGU_CHUNK), 0.5 * SGU_CHUNK ** -0.5),
        'sgu_b_s': gain(ks[9], (N_A, SGU_GROUPS, SGU_CHUNK)),
        'sgu_w_out': nrm(ks[10], (N_A, D_SGU, D_MODEL), D_SGU ** -0.5),
        'mla_w_in': nrm(ks[11], (N_B, D_MODEL, Q_LORA + KV_LORA + ROPE_DIM), D_MODEL ** -0.5),
        'mla_q_norm_g': gain(ks[12], (N_B, Q_LORA)),
        'mla_kv_norm_g': gain(ks[13], (N_B, KV_LORA)),
        'mla_w_uq': nrm(ks[14], (N_B, Q_LORA, N_HEADS * QK_DIM), Q_LORA ** -0.5),
        'mla_w_ukv': nrm(ks[15], (N_B, KV_LORA, N_HEADS * (NOPE_DIM + V_DIM)), KV_LORA ** -0.5),
        'mla_q_nope_g': gain(ks[16], (N_B, NOPE_DIM)),
        'mla_q_rope_g': gain(ks[17], (N_B, ROPE_DIM)),
        'mla_k_nope_g': gain(ks[18], (N_B, NOPE_DIM)),
        'mla_k_rope_g': gain(ks[19], (N_B, ROPE_DIM)),
        'mla_w_o': nrm(ks[20], (N_B, N_HEADS * V_DIM, D_MODEL), (N_HEADS * V_DIM) ** -0.5),
        'mlp_w_up': nrm(ks[21], (DEPTH, D_MODEL, D_FF), D_MODEL ** -0.5),
        'mlp_w_down': nrm(ks[22], (DEPTH, D_FF, D_MODEL), D_FF ** -0.5),
    }


def reference(x_prompt, x_sample, cache_kv_latent, cache_k_rope, mix_norm_g, mlp_norm_g,
              sgu_w_in, sgu_norm_g, sgu_w_s, sgu_b_s, sgu_w_out,
              mla_w_in, mla_q_norm_g, mla_kv_norm_g, mla_w_uq, mla_w_ukv,
              mla_q_nope_g, mla_q_rope_g, mla_k_nope_g, mla_k_rope_g, mla_w_o,
              mlp_w_up, mlp_w_down):
    n_prompt = x_prompt.shape[1]
    n_new = x_sample.shape[1]
    pos_prompt = jnp.arange(n_prompt)
    pos_sample = PAST_LEN + jnp.arange(n_new)
    xp, xs = x_prompt, x_sample
    lat_p, kr_p, lat_s, kr_s, sgu_v_s = [], [], [], [], []
    for i in range(DEPTH):
        j = i // N_MIXERS
        hp = rms_norm(xp, mix_norm_g[i])
        hs = rms_norm(xs, mix_norm_g[i])
        if i % N_MIXERS == 0:
            sgu_w = (sgu_w_in[j], sgu_norm_g[j], sgu_w_s[j], sgu_b_s[j], sgu_w_out[j])
            mp, _ = sgu_mixer(hp, *sgu_w, SGU_CHUNK)
            ms, v_new = sgu_mixer(hs, *sgu_w, n_new)
            sgu_v_s.append(v_new)
        else:
            proj = (mla_w_in[j], mla_q_norm_g[j], mla_kv_norm_g[j], mla_w_uq[j],
                    mla_q_nope_g[j], mla_q_rope_g[j], mla_k_rope_g[j])
            qn, qr, c, kr = mla_project(hp, pos_prompt, *proj)
            kn, v = mla_expand(c, mla_w_ukv[j], mla_k_nope_g[j])
            mp = mla_prompt_attention(qn, qr, kn, kr, v) @ mla_w_o[j]
            lat_p.append(c)
            kr_p.append(kr)
            qn, qr, c, kr = mla_project(hs, pos_sample, *proj)
            c_all = jnp.concatenate([cache_kv_latent[j], c], axis=1)
            kr_all = jnp.concatenate([cache_k_rope[j], kr], axis=1)
            kn, v = mla_expand(c_all, mla_w_ukv[j], mla_k_nope_g[j])
            o = mla_attend(qn, qr, kn, kr_all, v, None)
            ms = o.reshape(o.shape[0], n_new, N_HEADS * V_DIM) @ mla_w_o[j]
            lat_s.append(c)
            kr_s.append(kr)
        xp = xp + mp
        xs = xs + ms
        xp = xp + sqrelu_mlp(rms_norm(xp, mlp_norm_g[i]), mlp_w_up[i], mlp_w_down[i])
        xs = xs + sqrelu_mlp(rms_norm(xs, mlp_norm_g[i]), mlp_w_up[i], mlp_w_down[i])
    new_kv_latent_prompt = jnp.stack(lat_p)
    new_k_rope_prompt = jnp.stack(kr_p)
    new_kv_latent_sample = jnp.stack(lat_s)
    new_k_rope_sample = jnp.stack(kr_s)
    new_sgu_v_sample = jnp.stack(sgu_v_s)
    return (xp, xs, new_kv_latent_prompt, new_k_rope_prompt, new_kv_latent_sample, new_k_rope_sample, new_sgu_v_sample)
```

```python
import functools
import math

import jax
import jax.numpy as jnp
import numpy as np
from jax import lax
from jax.experimental import pallas as pl
from jax.experimental.pallas import tpu as pltpu

F32 = jnp.float32
BF16 = jnp.bfloat16

D_MODEL = 1024
CHUNK = 64
SGU_CHUNK = 128
D_SGU = 2 * D_MODEL
SGU_GROUP_DIM = 128
SGU_GROUPS = D_SGU // SGU_GROUP_DIM
N_HEADS = 16
Q_LORA = 384
KV_LORA = 256
NOPE_DIM = 64
ROPE_DIM = 32
HALF_ROPE = ROPE_DIM // 2
V_DIM = 64
QK_DIM = NOPE_DIM + ROPE_DIM
ATTN_SCALE = 1.0 / math.sqrt(QK_DIM)
ROPE_BASE = 10000.0
D_FF = 4 * D_MODEL
EPS = 1e-6
NEG_INF = -1e30

LANES = 128
HEAD_PAD = LANES
FF_CHUNK = 1024
SGU_COL_CHUNK = 512
VMEM_LIMIT = 56 * 1024 * 1024


def _params(n_axes):
    return pltpu.CompilerParams(
        dimension_semantics=("arbitrary",) * n_axes, vmem_limit_bytes=VMEM_LIMIT)


def _const_spec(shape):
    nd = len(shape)
    return pl.BlockSpec(shape, lambda *_: (0,) * nd, pipeline_mode=pl.Buffered(1))


def _row_tile(rows, want):
    tm = min(rows, want)
    assert rows % tm == 0
    return tm


def _rms(xf, g, n):
    ss = jnp.sum(xf * xf, axis=-1, keepdims=True)
    return xf * lax.rsqrt(ss * (1.0 / n) + EPS) * g


def _gelu(z):
    return 0.5 * z * (1.0 + lax.erf(z * np.float32(math.sqrt(0.5))))


def _mlp_delta(x, g_ref, wu_ref, wd_ref):
    h = _rms(x, g_ref[...], D_MODEL).astype(BF16)
    acc = None
    for f in range(D_FF // FF_CHUNK):
        sl = slice(f * FF_CHUNK, (f + 1) * FF_CHUNK)
        a = jnp.dot(h, wu_ref[:, sl], preferred_element_type=F32)
        a = jnp.square(jnp.maximum(a, 0.0)).astype(BF16)
        d = jnp.dot(a, wd_ref[sl, :], preferred_element_type=F32)
        acc = d if acc is None else acc + d
    return acc


def _mlp_kernel(x_ref, g_ref, wu_ref, wd_ref, o_ref):
    x = x_ref[...]
    o_ref[...] = x + _mlp_delta(x, g_ref, wu_ref, wd_ref)


def _attn_out_mlp_kernel(x_ref, a_ref, wo_ref, g_ref, wu_ref, wd_ref, o_ref):
    x = x_ref[...] + jnp.dot(a_ref[...], wo_ref[...], preferred_element_type=F32)
    o_ref[...] = x + _mlp_delta(x, g_ref, wu_ref, wd_ref)


def _mlp_call(x, g, wu, wd, attn=None, wo=None, tm_want=512):
    rows = x.shape[0]
    tm = _row_tile(rows, tm_want)
    row_spec = pl.BlockSpec((tm, D_MODEL), lambda i: (i, 0))
    w_specs = [_const_spec((1, D_MODEL)), _const_spec((D_MODEL, D_FF)), _const_spec((D_FF, D_MODEL))]
    if attn is None:
        kern, ins, specs = _mlp_kernel, (x, g, wu, wd), [row_spec] + w_specs
        name = "channel_mlp"
    else:
        kern, ins = _attn_out_mlp_kernel, (x, attn, wo, g, wu, wd)
        specs = [row_spec, row_spec, _const_spec((D_MODEL, D_MODEL))] + w_specs
        name = "attn_out_channel_mlp"
    return pl.pallas_call(
        kern, grid=(rows // tm,), in_specs=specs, out_specs=row_spec,
        out_shape=jax.ShapeDtypeStruct((rows, D_MODEL), F32),
        compiler_params=_params(1), name=name)(*ins)


def _sgu_kernel(x_ref, g_ref, win_ref, ng_ref, ws_ref, b_ref, wout_ref, *rest, block_diag, emit_v):
    if emit_v:
        o_ref, v_ref, u_scr, v_scr, y_scr = rest
    else:
        o_ref, u_scr, v_scr, y_scr = rest
        v_ref = None
    tm = x_ref.shape[0]
    n_chunks = tm // SGU_CHUNK
    x = x_ref[...]
    h = _rms(x, g_ref[...], D_MODEL).astype(BF16)

    ss = jnp.zeros((tm, 1), F32)
    for k in range(D_SGU // SGU_COL_CHUNK):
        sl = slice(k * SGU_COL_CHUNK, (k + 1) * SGU_COL_CHUNK)
        u_scr[:, sl] = _gelu(jnp.dot(h, win_ref[:, sl], preferred_element_type=F32))
        slv = slice(D_SGU + k * SGU_COL_CHUNK, D_SGU + (k + 1) * SGU_COL_CHUNK)
        zv = _gelu(jnp.dot(h, win_ref[:, slv], preferred_element_type=F32))
        ss = ss + jnp.sum(zv * zv, axis=-1, keepdims=True)
        v_scr[:, sl] = zv
    r = lax.rsqrt(ss * (1.0 / D_SGU) + EPS)

    ri = lax.broadcasted_iota(jnp.int32, (SGU_CHUNK, SGU_CHUNK), 0) // CHUNK
    ci = lax.broadcasted_iota(jnp.int32, (SGU_CHUNK, SGU_CHUNK), 1) // CHUNK
    mask = (ri == ci) if block_diag else (ri >= ci)
    for grp in range(SGU_GROUPS):
        gsl = slice(grp * SGU_GROUP_DIM, (grp + 1) * SGU_GROUP_DIM)
        ws = jnp.where(mask, ws_ref[grp], 0.0).astype(BF16)
        pieces = []
        for c in range(n_chunks):
            rsl = slice(c * SGU_CHUNK, (c + 1) * SGU_CHUNK)
            vn = v_scr[rsl, gsl] * r[rsl] * ng_ref[:, gsl]
            if emit_v:
                v_ref[rsl, gsl] = vn
            pieces.append(vn.astype(BF16))
        rhs = pieces[0] if n_chunks == 1 else jnp.concatenate(pieces, axis=1)
        mixed = jnp.dot(ws, rhs, preferred_element_type=F32) + b_ref[grp]
        for c in range(n_chunks):
            rsl = slice(c * SGU_CHUNK, (c + 1) * SGU_CHUNK)
            y = u_scr[rsl, gsl] * mixed[:, c * SGU_CHUNK:(c + 1) * SGU_CHUNK]
            y_scr[rsl, gsl] = y.astype(BF16)
    o_ref[...] = x + jnp.dot(y_scr[...], wout_ref[...], preferred_element_type=F32)


def _sgu_call(x, g, win, ng, ws, b, wout, *, block_diag, emit_v, tm_want=512):
    rows = x.shape[0]
    tm = _row_tile(rows, tm_want)
    row_spec = pl.BlockSpec((tm, D_MODEL), lambda i: (i, 0))
    specs = [row_spec, _const_spec((1, D_MODEL)), _const_spec((D_MODEL, 2 * D_SGU)),
             _const_spec((1, D_SGU)), _const_spec((SGU_GROUPS, SGU_CHUNK, SGU_CHUNK)),
             _const_spec((SGU_GROUPS, SGU_CHUNK, 1)), _const_spec((D_SGU, D_MODEL))]
    out_shape = [jax.ShapeDtypeStruct((rows, D_MODEL), F32)]
    out_specs = [row_spec]
    if emit_v:
        out_shape.append(jax.ShapeDtypeStruct((rows, D_SGU), F32))
        out_specs.append(pl.BlockSpec((tm, D_SGU), lambda i: (i, 0)))
    res = pl.pallas_call(
        functools.partial(_sgu_kernel, block_diag=block_diag, emit_v=emit_v),
        grid=(rows // tm,), in_specs=specs, out_specs=out_specs, out_shape=out_shape,
        scratch_shapes=[pltpu.VMEM((tm, D_SGU), F32), pltpu.VMEM((tm, D_SGU), F32),
                        pltpu.VMEM((tm, D_SGU), BF16)],
        compiler_params=_params(1), name="sgu_mixer_v" if emit_v else "sgu_mixer")(
            x, g, win, ng, ws, b, wout)
    return res if emit_v else res[0]


W_IN_PAD = Q_LORA + KV_LORA + 2 * LANES
Q_PAD = N_HEADS * HEAD_PAD


def _mla_proj_kernel(x_ref, g_ref, win_ref, qg_ref, kvg_ref, krg_ref, krgs_ref, wuq_ref,
                     qhg_ref, qhgs_ref, cos_ref, sin_ref, ckv_ref, kr_ref, krpad_ref, q_ref):
    x = x_ref[...]
    h = _rms(x, g_ref[...], D_MODEL).astype(BF16)
    a = jnp.dot(h, win_ref[...], preferred_element_type=F32)
    c_q = _rms(a[:, :Q_LORA], qg_ref[...], Q_LORA)
    ckv_ref[...] = _rms(a[:, Q_LORA:Q_LORA + KV_LORA], kvg_ref[...], KV_LORA)

    cos = cos_ref[...]
    sin = sin_ref[...]
    kr = a[:, Q_LORA + KV_LORA:Q_LORA + KV_LORA + LANES]
    kr_sw = a[:, Q_LORA + KV_LORA + LANES:]
    r_kr = lax.rsqrt(jnp.sum(kr * kr, axis=-1, keepdims=True) * (1.0 / ROPE_DIM) + EPS)
    kr_rot = (kr * r_kr * krg_ref[...]) * cos + (kr_sw * r_kr * krgs_ref[...]) * sin
    krpad_ref[...] = kr_rot.astype(BF16)
    kr_ref[...] = kr_rot[:, NOPE_DIM:NOPE_DIM + ROPE_DIM]

    qq = jnp.dot(c_q.astype(BF16), wuq_ref[...], preferred_element_type=F32)
    lane = lax.broadcasted_iota(jnp.int32, (1, HEAD_PAD), 1)
    is_nope = lane < NOPE_DIM
    for hd in range(N_HEADS):
        blk = qq[:, hd * HEAD_PAD:(hd + 1) * HEAD_PAD]
        blk_sw = qq[:, Q_PAD + hd * HEAD_PAD:Q_PAD + (hd + 1) * HEAD_PAD]
        sq = blk * blk
        s_n = jnp.sum(jnp.where(is_nope, sq, 0.0), axis=-1, keepdims=True)
        s_r = jnp.sum(jnp.where(is_nope, 0.0, sq), axis=-1, keepdims=True)
        r_n = lax.rsqrt(s_n * (1.0 / NOPE_DIM) + EPS)
        r_r = lax.rsqrt(s_r * (1.0 / ROPE_DIM) + EPS)
        qn = blk * jnp.where(is_nope, r_n, r_r) * qhg_ref[...]
        qs = blk_sw * r_r * qhgs_ref[...]
        q_ref[hd] = (qn * cos + qs * sin).astype(BF16)


def _mla_proj_call(x, g, win, qg, kvg, krg, krgs, wuq, qhg, qhgs, cos_t, sin_t, tm_want=512):
    rows = x.shape[0]
    period = cos_t.shape[0]
    tm = _row_tile(min(rows, period), tm_want)
    n_per = period // tm
    row = lambda w: pl.BlockSpec((tm, w), lambda i: (i, 0))
    tab = pl.BlockSpec((tm, HEAD_PAD), lambda i: (i % n_per, 0))
    specs = [row(D_MODEL), _const_spec((1, D_MODEL)), _const_spec((D_MODEL, W_IN_PAD)),
             _const_spec((1, Q_LORA)), _const_spec((1, KV_LORA)), _const_spec((1, LANES)),
             _const_spec((1, LANES)), _const_spec((Q_LORA, 2 * Q_PAD)), _const_spec((1, HEAD_PAD)),
             _const_spec((1, HEAD_PAD)), tab, tab]
    out_shape = [jax.ShapeDtypeStruct((rows, KV_LORA), F32),
                 jax.ShapeDtypeStruct((rows, ROPE_DIM), F32),
                 jax.ShapeDtypeStruct((rows, LANES), BF16),
                 jax.ShapeDtypeStruct((N_HEADS, rows, HEAD_PAD), BF16)]
    out_specs = [row(KV_LORA), row(ROPE_DIM), row(LANES),
                 pl.BlockSpec((N_HEADS, tm, HEAD_PAD), lambda i: (0, i, 0))]
    return pl.pallas_call(
        _mla_proj_kernel, grid=(rows // tm,), in_specs=specs, out_specs=out_specs,
        out_shape=out_shape, compiler_params=_params(1), name="mla_project")(
            x, g, win, qg, kvg, krg, krgs, wuq, qhg, qhgs, cos_t, sin_t)


def _kv_expand_kernel(c_ref, krpad_ref, wk_ref, wv_ref, kg_ref, k_ref, v_ref):
    c = c_ref[...].astype(BF16)
    kk = jnp.dot(c, wk_ref[...], preferred_element_type=F32)
    vv = jnp.dot(c, wv_ref[...], preferred_element_type=F32)
    krp = krpad_ref[...].astype(F32)
    for hd in range(N_HEADS):
        blk = kk[:, hd * HEAD_PAD:(hd + 1) * HEAD_PAD]
        r = lax.rsqrt(jnp.sum(blk * blk, axis=-1, keepdims=True) * (1.0 / NOPE_DIM) + EPS)
        k_ref[hd] = (blk * r * kg_ref[...] + krp).astype(BF16)
        v_ref[hd] = vv[:, hd * HEAD_PAD:(hd + 1) * HEAD_PAD].astype(BF16)


def _kv_expand_call(c, krpad, wk, wv, kg, tm_want=512):
    rows = c.shape[0]
    tm = _row_tile(rows, tm_want)
    row = lambda w: pl.BlockSpec((tm, w), lambda i: (i, 0))
    head = pl.BlockSpec((N_HEADS, tm, HEAD_PAD), lambda i: (0, i, 0))
    hs = jax.ShapeDtypeStruct((N_HEADS, rows, HEAD_PAD), BF16)
    return pl.pallas_call(
        _kv_expand_kernel, grid=(rows // tm,),
        in_specs=[row(KV_LORA), row(LANES), _const_spec((KV_LORA, Q_PAD)),
                  _const_spec((KV_LORA, Q_PAD)), _const_spec((1, HEAD_PAD))],
        out_specs=[head, head], out_shape=[hs, hs],
        compiler_params=_params(1), name="kv_expand")(c, krpad, wk, wv, kg)


def _online_step(q, k, v, m, l, acc, mask):
    s = lax.dot_general(q, k, (((1,), (1,)), ((), ())), preferred_element_type=F32)
    if mask is not None:
        s = jnp.where(mask, s, NEG_INF)
    m_new = jnp.maximum(m, jnp.max(s, axis=-1, keepdims=True))
    alpha = jnp.exp(m - m_new)
    p = jnp.exp(s - m_new)
    l = alpha * l + jnp.sum(p, axis=-1, keepdims=True)
    acc = alpha * acc + jnp.dot(p.astype(BF16), v, preferred_element_type=F32)
    return m_new, l, acc


def _prompt_attn_kernel(q_ref, k_ref, v_ref, o_ref, *, tq, tk):
    qi = pl.program_id(1)
    n_full = (qi * tq) // tk
    ri = (lax.broadcasted_iota(jnp.int32, (tq, tk), 0)) // CHUNK
    ci = (lax.broadcasted_iota(jnp.int32, (tq, tk), 1)) // CHUNK
    diag_mask = ci <= ri
    for pair in range(N_HEADS // 2):
        out = None
        for hd in (2 * pair, 2 * pair + 1):
            q = q_ref[hd]

            def body(kb, carry, q=q, hd=hd):
                ks = pl.ds(pl.multiple_of(kb * tk, tk), tk)
                return _online_step(q, k_ref[hd, ks, :], v_ref[hd, ks, :], *carry, None)

            init = (jnp.full((tq, 1), NEG_INF, F32), jnp.zeros((tq, 1), F32),
                    jnp.zeros((tq, HEAD_PAD), F32))
            carry = lax.fori_loop(0, n_full, body, init)
            ks = pl.ds(pl.multiple_of(n_full * tk, tk), tk)
            m, l, acc = _online_step(q, k_ref[hd, ks, :], v_ref[hd, ks, :], *carry, diag_mask)
            o = acc / l
            out = o if out is None else out + o
        o_ref[:, pair * LANES:(pair + 1) * LANES] = out.astype(BF16)


def _prompt_attn_call(q, k, v, batch, seq, tq=512):
    tq = min(tq, seq)
    nq = seq // tq
    rows = batch * seq
    return pl.pallas_call(
        functools.partial(_prompt_attn_kernel, tq=tq, tk=tq),
        grid=(batch, nq),
        in_specs=[pl.BlockSpec((N_HEADS, tq, HEAD_PAD), lambda b, i: (0, b * nq + i, 0)),
                  pl.BlockSpec((N_HEADS, seq, HEAD_PAD), lambda b, i: (0, b, 0)),
                  pl.BlockSpec((N_HEADS, seq, HEAD_PAD), lambda b, i: (0, b, 0))],
        out_specs=pl.BlockSpec((tq, D_MODEL), lambda b, i: (b * nq + i, 0)),
        out_shape=jax.ShapeDtypeStruct((rows, D_MODEL), BF16),
        compiler_params=_params(2), name="prompt_attention")(q, k, v)


def _sample_attn_kernel(q_ref, kc_ref, vc_ref, kn_ref, vn_ref, o_ref, m_scr, l_scr, acc_scr):
    kb = pl.program_id(1)
    last = pl.num_programs(1) - 1

    @pl.when(kb == 0)
    def _():
        m_scr[...] = jnp.full(m_scr.shape, NEG_INF, F32)
        l_scr[...] = jnp.zeros(l_scr.shape, F32)
        acc_scr[...] = jnp.zeros(acc_scr.shape, F32)

    for hd in range(N_HEADS):
        m, l, acc = _online_step(q_ref[hd], kc_ref[hd], vc_ref[hd],
                                 m_scr[hd], l_scr[hd], acc_scr[hd], None)
        m_scr[hd] = m
        l_scr[hd] = l
        acc_scr[hd] = acc

    @pl.when(kb == last)
    def _():
        for pair in range(N_HEADS // 2):
            out = None
            for hd in (2 * pair, 2 * pair + 1):
                m, l, acc = _online_step(q_ref[hd], kn_ref[hd], vn_ref[hd],
                                         m_scr[hd], l_scr[hd], acc_scr[hd], None)
                o = acc / l
                out = o if out is None else out + o
            o_ref[:, pair * LANES:(pair + 1) * LANES] = out.astype(BF16)


def _sample_attn_call(q, kc, vc, kn, vn, batch, n_new, past, tk=512):
    tk = min(tk, past)
    nk = past // tk
    new_spec = pl.BlockSpec((N_HEADS, n_new, HEAD_PAD), lambda b, j: (0, b, 0))
    cache_spec = pl.BlockSpec((N_HEADS, tk, HEAD_PAD), lambda b, j: (0, b * nk + j, 0))
    return pl.pallas_call(
        _sample_attn_kernel, grid=(batch, nk),
        in_specs=[new_spec, cache_spec, cache_spec, new_spec, new_spec],
        out_specs=pl.BlockSpec((n_new, D_MODEL), lambda b, j: (b, 0)),
        out_shape=jax.ShapeDtypeStruct((batch * n_new, D_MODEL), BF16),
        scratch_shapes=[pltpu.VMEM((N_HEADS, n_new, 1), F32), pltpu.VMEM((N_HEADS, n_new, 1), F32),
                        pltpu.VMEM((N_HEADS, n_new, HEAD_PAD), F32)],
        compiler_params=_params(2), name="sample_attention")(q, kc, vc, kn, vn)


def _head_pad_cols(w, per_head, offset):
    k = w.shape[0]
    w3 = w.reshape(k, N_HEADS, per_head)
    out = jnp.zeros((k, N_HEADS, HEAD_PAD), w.dtype)
    if isinstance(offset, int):
        return out.at[:, :, offset:offset + per_head].set(w3).reshape(k, Q_PAD)
    for hd in range(N_HEADS):
        out = out.at[:, hd, offset[hd]:offset[hd] + per_head].set(w3[:, hd])
    return out.reshape(k, Q_PAD)


def _lane_vec(parts):
    v = jnp.concatenate(parts)
    return jnp.pad(v, (0, LANES - v.shape[0])).reshape(1, LANES)


def _rope_tables(pos):
    inv = ROPE_BASE ** (-jnp.arange(HALF_ROPE, dtype=F32) / HALF_ROPE)
    ang = pos.astype(F32)[:, None] * inv[None, :]
    cos, sin = jnp.cos(ang), jnp.sin(ang)
    n = pos.shape[0]
    ones = jnp.ones((n, NOPE_DIM), F32)
    zeros_n = jnp.zeros((n, NOPE_DIM), F32)
    zeros_t = jnp.zeros((n, HEAD_PAD - QK_DIM), F32)
    cos_t = jnp.concatenate([ones, cos, cos, zeros_t], axis=1)
    sin_t = jnp.concatenate([zeros_n, -sin, sin, zeros_t], axis=1)
    return cos_t, sin_t


def _prep_mla(w_in, q_norm_g, kv_norm_g, w_uq, w_ukv, q_nope_g, q_rope_g, k_nope_g, k_rope_g, w_o):
    zeros = lambda n: jnp.zeros((n,), F32)
    w_q, w_kv = w_in[:, :Q_LORA], w_in[:, Q_LORA:Q_LORA + KV_LORA]
    w_kr = w_in[:, Q_LORA + KV_LORA:]
    w_kr1, w_kr2 = w_kr[:, :HALF_ROPE], w_kr[:, HALF_ROPE:]
    zc = lambda n: jnp.zeros((D_MODEL, n), F32)
    tail = HEAD_PAD - QK_DIM
    win = jnp.concatenate([w_q, w_kv, zc(NOPE_DIM), w_kr1, w_kr2, zc(tail),
                           zc(NOPE_DIM), w_kr2, w_kr1, zc(tail)], axis=1).astype(BF16)
    krg = _lane_vec([zeros(NOPE_DIM), k_rope_g])
    krgs = _lane_vec([zeros(NOPE_DIM), k_rope_g[HALF_ROPE:], k_rope_g[:HALF_ROPE]])

    uq = w_uq.reshape(Q_LORA, N_HEADS, QK_DIM)
    uq_n, uq_1, uq_2 = uq[..., :NOPE_DIM], uq[..., NOPE_DIM:NOPE_DIM + HALF_ROPE], uq[..., NOPE_DIM + HALF_ROPE:]
    zq = lambda n: jnp.zeros((Q_LORA, N_HEADS, n), F32)
    main = jnp.concatenate([uq_n, uq_1, uq_2, zq(tail)], axis=-1).reshape(Q_LORA, Q_PAD)
    swap = jnp.concatenate([zq(NOPE_DIM), uq_2, uq_1, zq(tail)], axis=-1).reshape(Q_LORA, Q_PAD)
    wuq = jnp.concatenate([main, swap], axis=1).astype(BF16)
    qhg = _lane_vec([q_nope_g, q_rope_g]) * ATTN_SCALE
    qhgs = _lane_vec([zeros(NOPE_DIM), q_rope_g[HALF_ROPE:], q_rope_g[:HALF_ROPE]]) * ATTN_SCALE

    ukv = w_ukv.reshape(KV_LORA, N_HEADS, NOPE_DIM + V_DIM)
    wk = _head_pad_cols(ukv[..., :NOPE_DIM].reshape(KV_LORA, -1), NOPE_DIM, 0).astype(BF16)
    v_off = [V_DIM * (hd % 2) for hd in range(N_HEADS)]
    wv = _head_pad_cols(ukv[..., NOPE_DIM:].reshape(KV_LORA, -1), V_DIM, v_off).astype(BF16)
    kg = _lane_vec([k_nope_g])
    return dict(win=win, qg=q_norm_g.reshape(1, -1), kvg=kv_norm_g.reshape(1, -1), krg=krg, krgs=krgs,
                wuq=wuq, qhg=qhg, qhgs=qhgs, wk=wk, wv=wv, kg=kg, wo=w_o.astype(BF16))


def kernel(x_prompt, x_sample, cache_kv_latent, cache_k_rope, mix_norm_g, mlp_norm_g, sgu_w_in, sgu_norm_g, sgu_w_s, sgu_b_s, sgu_w_out, mla_w_in, mla_q_norm_g, mla_kv_norm_g, mla_w_uq, mla_w_ukv, mla_q_nope_g, mla_q_rope_g, mla_k_nope_g, mla_k_rope_g, mla_w_o, mlp_w_up, mlp_w_down):
    batch, seq, _ = x_prompt.shape
    dec_batch, n_new, _ = x_sample.shape
    past = cache_kv_latent.shape[2]
    depth = mix_norm_g.shape[0]
    rows_p, rows_s = batch * seq, dec_batch * n_new

    xp = x_prompt.reshape(rows_p, D_MODEL)
    xs = x_sample.reshape(rows_s, D_MODEL)
    cos_p, sin_p = _rope_tables(jnp.arange(seq))
    cos_s, sin_s = _rope_tables(past + jnp.arange(n_new))
    cos_s, sin_s = jnp.tile(cos_s, (dec_batch, 1)), jnp.tile(sin_s, (dec_batch, 1))

    lat_p, kr_p, lat_s, kr_s, sgu_v_s = [], [], [], [], []
    for i in range(depth):
        j = i // 2
        g_mix = mix_norm_g[i].reshape(1, D_MODEL)
        g_mlp = mlp_norm_g[i].reshape(1, D_MODEL)
        wu = mlp_w_up[i].astype(BF16)
        wd = mlp_w_down[i].astype(BF16)
        if i % 2 == 0:
            win = sgu_w_in[j].astype(BF16)
            wout = sgu_w_out[j].astype(BF16)
            ng = sgu_norm_g[j].reshape(1, D_SGU)
            ws_p = sgu_w_s[j]
            b_p = sgu_b_s[j].reshape(SGU_GROUPS, SGU_CHUNK, 1)
            reps = SGU_CHUNK // n_new
            ws_s = jnp.tile(sgu_w_s[j][:, :n_new, :n_new], (1, reps, reps))
            b_s = jnp.tile(sgu_b_s[j][:, :n_new], (1, reps)).reshape(SGU_GROUPS, SGU_CHUNK, 1)
            xp = _sgu_call(xp, g_mix, win, ng, ws_p, b_p, wout, block_diag=False, emit_v=False)
            xs, v_new = _sgu_call(xs, g_mix, win, ng, ws_s, b_s, wout, block_diag=True, emit_v=True)
            sgu_v_s.append(v_new.reshape(dec_batch, n_new, D_SGU))
            xp = _mlp_call(xp, g_mlp, wu, wd)
            xs = _mlp_call(xs, g_mlp, wu, wd)
        else:
            w = _prep_mla(mla_w_in[j], mla_q_norm_g[j], mla_kv_norm_g[j], mla_w_uq[j], mla_w_ukv[j],
                          mla_q_nope_g[j], mla_q_rope_g[j], mla_k_nope_g[j], mla_k_rope_g[j], mla_w_o[j])
            proj = lambda x, c_t, s_t: _mla_proj_call(
                x, g_mix, w["win"], w["qg"], w["kvg"], w["krg"], w["krgs"], w["wuq"],
                w["qhg"], w["qhgs"], c_t, s_t)
            c, kr, krpad, q = proj(xp, cos_p, sin_p)
            k, v = _kv_expand_call(c, krpad, w["wk"], w["wv"], w["kg"])
            attn = _prompt_attn_call(q, k, v, batch, seq)
            xp = _mlp_call(xp, g_mlp, wu, wd, attn=attn, wo=w["wo"])
            lat_p.append(c.reshape(batch, seq, KV_LORA))
            kr_p.append(kr.reshape(batch, seq, ROPE_DIM))
            c, kr, krpad, q = proj(xs, cos_s, sin_s)
            kn, vn = _kv_expand_call(c, krpad, w["wk"], w["wv"], w["kg"])
            cache_c = cache_kv_latent[j].reshape(dec_batch * past, KV_LORA)
            cache_kr = jnp.pad(cache_k_rope[j].reshape(dec_batch * past, ROPE_DIM),
                               ((0, 0), (NOPE_DIM, HEAD_PAD - QK_DIM))).astype(BF16)
            kc, vc = _kv_expand_call(cache_c, cache_kr, w["wk"], w["wv"], w["kg"])
            attn = _sample_attn_call(q, kc, vc, kn, vn, dec_batch, n_new, past)
            xs = _mlp_call(xs, g_mlp, wu, wd, attn=attn, wo=w["wo"])
            lat_s.append(c.reshape(dec_batch, n_new, KV_LORA))
            kr_s.append(kr.reshape(dec_batch, n_new, ROPE_DIM))
    return (xp.reshape(batch, seq, D_MODEL), xs.reshape(dec_batch, n_new, D_MODEL),
            jnp.stack(lat_p), jnp.stack(kr_p), jnp.stack(lat_s), jnp.stack(kr_s), jnp.stack(sgu_v_s))
```

```python
import functools
import math

import jax
import jax.numpy as jnp
import numpy as np
from jax import lax
from jax.experimental import pallas as pl
from jax.experimental.pallas import tpu as pltpu

F32 = jnp.float32
BF16 = jnp.bfloat16

D_MODEL = 1024
CHUNK = 64
SGU_CHUNK = 128
D_SGU = 2 * D_MODEL
SGU_GROUP_DIM = 128
SGU_GROUPS = D_SGU // SGU_GROUP_DIM
N_HEADS = 16
Q_LORA = 384
KV_LORA = 256
NOPE_DIM = 64
ROPE_DIM = 32
HALF_ROPE = ROPE_DIM // 2
V_DIM = 64
QK_DIM = NOPE_DIM + ROPE_DIM
ATTN_SCALE = 1.0 / math.sqrt(QK_DIM)
ROPE_BASE = 10000.0
D_FF = 4 * D_MODEL
EPS = 1e-6
NEG_INF = -1e30

LANES = 128
HEAD_PAD = LANES
FF_CHUNK = 1024
SGU_COL_CHUNK = 512
VMEM_LIMIT = 56 * 1024 * 1024


def _params(n_axes):
    return pltpu.CompilerParams(
        dimension_semantics=("arbitrary",) * n_axes, vmem_limit_bytes=VMEM_LIMIT)


def _const_spec(shape):
    nd = len(shape)
    return pl.BlockSpec(shape, lambda *_: (0,) * nd, pipeline_mode=pl.Buffered(1))


def _row_tile(rows, want):
    tm = min(rows, want)
    assert rows % tm == 0
    return tm


def _rms(xf, g, n):
    ss = jnp.sum(xf * xf, axis=-1, keepdims=True)
    return xf * lax.rsqrt(ss * (1.0 / n) + EPS) * g


def _gelu(z):
    return 0.5 * z * (1.0 + lax.erf(z * np.float32(math.sqrt(0.5))))


def _mlp_delta(x, g_ref, wu_ref, wd_ref):
    h = _rms(x, g_ref[...], D_MODEL).astype(BF16)
    acc = None
    for f in range(D_FF // FF_CHUNK):
        sl = slice(f * FF_CHUNK, (f + 1) * FF_CHUNK)
        a = jnp.dot(h, wu_ref[:, sl], preferred_element_type=F32)
        a = jnp.square(jnp.maximum(a, 0.0)).astype(BF16)
        d = jnp.dot(a, wd_ref[sl, :], preferred_element_type=F32)
        acc = d if acc is None else acc + d
    return acc


def _mlp_kernel(x_ref, g_ref, wu_ref, wd_ref, o_ref):
    x = x_ref[...]
    o_ref[...] = x + _mlp_delta(x, g_ref, wu_ref, wd_ref)


def _attn_out_mlp_kernel(x_ref, a_ref, wo_ref, g_ref, wu_ref, wd_ref, o_ref):
    x = x_ref[...] + jnp.dot(a_ref[...], wo_ref[...], preferred_element_type=F32)
    o_ref[...] = x + _mlp_delta(x, g_ref, wu_ref, wd_ref)


def _mlp_call(x, g, wu, wd, attn=None, wo=None, tm_want=512):
    rows = x.shape[0]
    tm = _row_tile(rows, tm_want)
    row_spec = pl.BlockSpec((tm, D_MODEL), lambda i: (i, 0))
    w_specs = [_const_spec((1, D_MODEL)), _const_spec((D_MODEL, D_FF)), _const_spec((D_FF, D_MODEL))]
    if attn is None:
        kern, ins, specs = _mlp_kernel, (x, g, wu, wd), [row_spec] + w_specs
        name = "channel_mlp"
    else:
        kern, ins = _attn_out_mlp_kernel, (x, attn, wo, g, wu, wd)
        specs = [row_spec, row_spec, _const_spec((D_MODEL, D_MODEL))] + w_specs
        name = "attn_out_channel_mlp"
    return pl.pallas_call(
        kern, grid=(rows // tm,), in_specs=specs, out_specs=row_spec,
        out_shape=jax.ShapeDtypeStruct((rows, D_MODEL), F32),
        compiler_params=_params(1), name=name)(*ins)


def _sgu_kernel(x_ref, g_ref, win_ref, ng_ref, ws_ref, b_ref, wout_ref, *rest, block_diag, emit_v):
    if emit_v:
        o_ref, v_ref, u_scr, v_scr, y_scr = rest
    else:
        o_ref, u_scr, v_scr, y_scr = rest
        v_ref = None
    tm = x_ref.shape[0]
    n_chunks = tm // SGU_CHUNK
    x = x_ref[...]
    h = _rms(x, g_ref[...], D_MODEL).astype(BF16)

    ss = jnp.zeros((tm, 1), F32)
    for k in range(D_SGU // SGU_COL_CHUNK):
        sl = slice(k * SGU_COL_CHUNK, (k + 1) * SGU_COL_CHUNK)
        u_scr[:, sl] = _gelu(jnp.dot(h, win_ref[:, sl], preferred_element_type=F32))
        slv = slice(D_SGU + k * SGU_COL_CHUNK, D_SGU + (k + 1) * SGU_COL_CHUNK)
        zv = _gelu(jnp.dot(h, win_ref[:, slv], preferred_element_type=F32))
        ss = ss + jnp.sum(zv * zv, axis=-1, keepdims=True)
        v_scr[:, sl] = zv
    r = lax.rsqrt(ss * (1.0 / D_SGU) + EPS)

    ri = lax.broadcasted_iota(jnp.int32, (SGU_CHUNK, SGU_CHUNK), 0) // CHUNK
    ci = lax.broadcasted_iota(jnp.int32, (SGU_CHUNK, SGU_CHUNK), 1) // CHUNK
    mask = (ri == ci) if block_diag else (ri >= ci)
    for grp in range(SGU_GROUPS):
        gsl = slice(grp * SGU_GROUP_DIM, (grp + 1) * SGU_GROUP_DIM)
        ws = jnp.where(mask, ws_ref[grp], 0.0).astype(BF16)
        pieces = []
        for c in range(n_chunks):
            rsl = slice(c * SGU_CHUNK, (c + 1) * SGU_CHUNK)
            vn = v_scr[rsl, gsl] * r[rsl] * ng_ref[:, gsl]
            if emit_v:
                v_ref[rsl, gsl] = vn
            pieces.append(vn.astype(BF16))
        rhs = pieces[0] if n_chunks == 1 else jnp.concatenate(pieces, axis=1)
        mixed = jnp.dot(ws, rhs, preferred_element_type=F32) + b_ref[grp]
        for c in range(n_chunks):
            rsl = slice(c * SGU_CHUNK, (c + 1) * SGU_CHUNK)
            y = u_scr[rsl, gsl] * mixed[:, c * SGU_CHUNK:(c + 1) * SGU_CHUNK]
            y_scr[rsl, gsl] = y.astype(BF16)
    o_ref[...] = x + jnp.dot(y_scr[...], wout_ref[...], preferred_element_type=F32)


def _sgu_call(x, g, win, ng, ws, b, wout, *, block_diag, emit_v, tm_want=512):
    rows = x.shape[0]
    tm = _row_tile(rows, tm_want)
    row_spec = pl.BlockSpec((tm, D_MODEL), lambda i: (i, 0))
    specs = [row_spec, _const_spec((1, D_MODEL)), _const_spec((D_MODEL, 2 * D_SGU)),
             _const_spec((1, D_SGU)), _const_spec((SGU_GROUPS, SGU_CHUNK, SGU_CHUNK)),
             _const_spec((SGU_GROUPS, SGU_CHUNK, 1)), _const_spec((D_SGU, D_MODEL))]
    out_shape = [jax.ShapeDtypeStruct((rows, D_MODEL), F32)]
    out_specs = [row_spec]
    if emit_v:
        out_shape.append(jax.ShapeDtypeStruct((rows, D_SGU), F32))
        out_specs.append(pl.BlockSpec((tm, D_SGU), lambda i: (i, 0)))
    res = pl.pallas_call(
        functools.partial(_sgu_kernel, block_diag=block_diag, emit_v=emit_v),
        grid=(rows // tm,), in_specs=specs, out_specs=out_specs, out_shape=out_shape,
        scratch_shapes=[pltpu.VMEM((tm, D_SGU), F32), pltpu.VMEM((tm, D_SGU), F32),
                        pltpu.VMEM((tm, D_SGU), BF16)],
        compiler_params=_params(1), name="sgu_mixer_v" if emit_v else "sgu_mixer")(
            x, g, win, ng, ws, b, wout)
    return res if emit_v else res[0]


W_IN_PAD = Q_LORA + KV_LORA + 2 * LANES
Q_PAD = N_HEADS * HEAD_PAD
LOG2E = math.log2(math.e)


def _ones_lane(hd):
    return V_DIM if hd % 2 == 0 else 0


def _mla_proj_kernel(x_ref, g_ref, win_ref, qg_ref, kvg_ref, wuq_ref, qcos_ref, qsin_ref,
                     kcos_ref, ksin_ref, ckv_ref, kr_ref, krpad_ref, q_ref):
    x = x_ref[...]
    h = _rms(x, g_ref[...], D_MODEL).astype(BF16)
    a = jnp.dot(h, win_ref[...], preferred_element_type=F32)
    c_q = _rms(a[:, :Q_LORA], qg_ref[...], Q_LORA)
    ckv_ref[...] = _rms(a[:, Q_LORA:Q_LORA + KV_LORA], kvg_ref[...], KV_LORA)

    kr = a[:, Q_LORA + KV_LORA:Q_LORA + KV_LORA + LANES]
    kr_sw = a[:, Q_LORA + KV_LORA + LANES:]
    r_kr = lax.rsqrt(jnp.sum(kr * kr, axis=-1, keepdims=True) * (1.0 / ROPE_DIM) + EPS)
    kr_rot = (kr * r_kr) * kcos_ref[...] + (kr_sw * r_kr) * ksin_ref[...]
    krpad_ref[...] = kr_rot.astype(BF16)
    kr_ref[...] = kr_rot[:, NOPE_DIM:NOPE_DIM + ROPE_DIM]

    qq = jnp.dot(c_q.astype(BF16), wuq_ref[...], preferred_element_type=F32)
    qcos = qcos_ref[...]
    qsin = qsin_ref[...]
    is_nope = lax.broadcasted_iota(jnp.int32, (1, HEAD_PAD), 1) < NOPE_DIM
    for hd in range(N_HEADS):
        blk = qq[:, hd * HEAD_PAD:(hd + 1) * HEAD_PAD]
        blk_sw = qq[:, Q_PAD + hd * HEAD_PAD:Q_PAD + (hd + 1) * HEAD_PAD]
        sq = blk * blk
        s_n = jnp.sum(jnp.where(is_nope, sq, 0.0), axis=-1, keepdims=True)
        s_r = jnp.sum(jnp.where(is_nope, 0.0, sq), axis=-1, keepdims=True)
        r_n = lax.rsqrt(s_n * (1.0 / NOPE_DIM) + EPS)
        r_r = lax.rsqrt(s_r * (1.0 / ROPE_DIM) + EPS)
        q_ref[hd] = ((blk * jnp.where(is_nope, r_n, r_r)) * qcos + (blk_sw * r_r) * qsin).astype(BF16)


def _mla_proj_call(x, g, win, qg, kvg, wuq, tables, tm_want=512):
    rows = x.shape[0]
    period = tables[0].shape[0]
    tm = _row_tile(min(rows, period), tm_want)
    n_per = period // tm
    row = lambda w: pl.BlockSpec((tm, w), lambda i: (i, 0))
    tab = pl.BlockSpec((tm, HEAD_PAD), lambda i: (i % n_per, 0))
    specs = [row(D_MODEL), _const_spec((1, D_MODEL)), _const_spec((D_MODEL, W_IN_PAD)),
             _const_spec((1, Q_LORA)), _const_spec((1, KV_LORA)), _const_spec((Q_LORA, 2 * Q_PAD)),
             tab, tab, tab, tab]
    out_shape = [jax.ShapeDtypeStruct((rows, KV_LORA), F32),
                 jax.ShapeDtypeStruct((rows, ROPE_DIM), F32),
                 jax.ShapeDtypeStruct((rows, LANES), BF16),
                 jax.ShapeDtypeStruct((N_HEADS, rows, HEAD_PAD), BF16)]
    out_specs = [row(KV_LORA), row(ROPE_DIM), row(LANES),
                 pl.BlockSpec((N_HEADS, tm, HEAD_PAD), lambda i: (0, i, 0))]
    return pl.pallas_call(
        _mla_proj_kernel, grid=(rows // tm,), in_specs=specs, out_specs=out_specs,
        out_shape=out_shape, compiler_params=_params(1), name="mla_project")(
            x, g, win, qg, kvg, wuq, *tables)


def _kv_expand_kernel(c_ref, krpad_ref, wk_ref, wv_ref, kg_ref, k_ref, v_ref):
    c = c_ref[...].astype(BF16)
    kk = jnp.dot(c, wk_ref[...], preferred_element_type=F32)
    vv = jnp.dot(c, wv_ref[...], preferred_element_type=F32)
    krp = krpad_ref[...].astype(F32)
    lane = lax.broadcasted_iota(jnp.int32, (1, HEAD_PAD), 1)
    for hd in range(N_HEADS):
        blk = kk[:, hd * HEAD_PAD:(hd + 1) * HEAD_PAD]
        r = lax.rsqrt(jnp.sum(blk * blk, axis=-1, keepdims=True) * (1.0 / NOPE_DIM) + EPS)
        k_ref[hd] = (blk * r * kg_ref[...] + krp).astype(BF16)
        vblk = vv[:, hd * HEAD_PAD:(hd + 1) * HEAD_PAD]
        v_ref[hd] = jnp.where(lane == _ones_lane(hd), 1.0, vblk).astype(BF16)


def _kv_expand_call(c, krpad, wk, wv, kg, tm_want=512):
    rows = c.shape[0]
    tm = _row_tile(rows, tm_want)
    row = lambda w: pl.BlockSpec((tm, w), lambda i: (i, 0))
    head = pl.BlockSpec((N_HEADS, tm, HEAD_PAD), lambda i: (0, i, 0))
    hs = jax.ShapeDtypeStruct((N_HEADS, rows, HEAD_PAD), BF16)
    return pl.pallas_call(
        _kv_expand_kernel, grid=(rows // tm,),
        in_specs=[row(KV_LORA), row(LANES), _const_spec((KV_LORA, Q_PAD)),
                  _const_spec((KV_LORA, Q_PAD)), _const_spec((1, HEAD_PAD))],
        out_specs=[head, head], out_shape=[hs, hs],
        compiler_params=_params(1), name="kv_expand")(c, krpad, wk, wv, kg)


def _scores(q, k):
    batch_dims = tuple(range(q.ndim - 2))
    return lax.dot_general(q, k, (((q.ndim - 1,), (k.ndim - 1,)), (batch_dims, batch_dims)),
                           preferred_element_type=F32)


def _online_update(s, v, m, acc):
    m_new = jnp.maximum(m, jnp.max(s, axis=-1, keepdims=True))
    alpha = jnp.exp2(m - m_new)
    p = jnp.exp2(s - m_new).astype(BF16)
    batch_dims = tuple(range(p.ndim - 2))
    pv = lax.dot_general(p, v, (((p.ndim - 1,), (v.ndim - 2,)), (batch_dims, batch_dims)),
                         preferred_element_type=F32)
    return m_new, alpha * acc + pv


def _pair_output(acc_even, acc_odd):
    l_even = acc_even[:, _ones_lane(0):_ones_lane(0) + 1]
    l_odd = acc_odd[:, _ones_lane(1):_ones_lane(1) + 1]
    lane = lax.broadcasted_iota(jnp.int32, (1, HEAD_PAD), 1)
    return jnp.where(lane < V_DIM, acc_even / l_even, acc_odd / l_odd).astype(BF16)


def _prompt_attn_kernel(q_ref, k_ref, v_ref, o_ref, *, tq, tk):
    qi = pl.program_id(1)
    n_full = (qi * tq) // tk
    ri = (lax.broadcasted_iota(jnp.int32, (tq, tk), 0)) // CHUNK
    ci = (lax.broadcasted_iota(jnp.int32, (tq, tk), 1)) // CHUNK
    diag_mask = ci <= ri
    for pair in range(N_HEADS // 2):
        heads = (2 * pair, 2 * pair + 1)
        qs = [q_ref[hd] for hd in heads]

        def step(kb, carry, mask, qs=qs, heads=heads):
            ks = pl.ds(pl.multiple_of(kb * tk, tk), tk)
            ss = [_scores(q, k_ref[hd, ks, :]) for q, hd in zip(qs, heads)]
            out = []
            for s, hd, (m, acc) in zip(ss, heads, carry):
                if mask is not None:
                    s = jnp.where(mask, s, NEG_INF)
                out.append(_online_update(s, v_ref[hd, ks, :], m, acc))
            return tuple(out)

        init = tuple((jnp.full((tq, 1), NEG_INF, F32), jnp.zeros((tq, HEAD_PAD), F32)) for _ in heads)
        carry = lax.fori_loop(0, n_full, functools.partial(step, mask=None), init)
        (_, acc_e), (_, acc_o) = step(n_full, carry, diag_mask)
        o_ref[:, pair * LANES:(pair + 1) * LANES] = _pair_output(acc_e, acc_o)


def _prompt_attn_call(q, k, v, batch, seq, tq=512):
    tq = min(tq, seq)
    nq = seq // tq
    rows = batch * seq
    return pl.pallas_call(
        functools.partial(_prompt_attn_kernel, tq=tq, tk=tq),
        grid=(batch, nq),
        in_specs=[pl.BlockSpec((N_HEADS, tq, HEAD_PAD), lambda b, i: (0, b * nq + i, 0)),
                  pl.BlockSpec((N_HEADS, seq, HEAD_PAD), lambda b, i: (0, b, 0)),
                  pl.BlockSpec((N_HEADS, seq, HEAD_PAD), lambda b, i: (0, b, 0))],
        out_specs=pl.BlockSpec((tq, D_MODEL), lambda b, i: (b * nq + i, 0)),
        out_shape=jax.ShapeDtypeStruct((rows, D_MODEL), BF16),
        compiler_params=_params(2), name="prompt_attention")(q, k, v)


def _sample_attn_kernel(q_ref, kc_ref, vc_ref, kn_ref, vn_ref, o_ref, m_scr, acc_scr):
    kb = pl.program_id(1)
    last = pl.num_programs(1) - 1

    @pl.when(kb == 0)
    def _():
        m_scr[...] = jnp.full(m_scr.shape, NEG_INF, F32)
        acc_scr[...] = jnp.zeros(acc_scr.shape, F32)

    q = q_ref[...]
    m, acc = _online_update(_scores(q, kc_ref[...]), vc_ref[...], m_scr[...], acc_scr[...])
    m_scr[...] = m
    acc_scr[...] = acc

    @pl.when(kb == last)
    def _():
        _, acc_f = _online_update(_scores(q, kn_ref[...]), vn_ref[...], m, acc)
        for pair in range(N_HEADS // 2):
            o_ref[:, pair * LANES:(pair + 1) * LANES] = _pair_output(acc_f[2 * pair], acc_f[2 * pair + 1])


def _sample_attn_call(q, kc, vc, kn, vn, batch, n_new, past, tk=512):
    tk = min(tk, past)
    nk = past // tk
    new_spec = pl.BlockSpec((N_HEADS, n_new, HEAD_PAD), lambda b, j: (0, b, 0))
    cache_spec = pl.BlockSpec((N_HEADS, tk, HEAD_PAD), lambda b, j: (0, b * nk + j, 0))
    return pl.pallas_call(
        _sample_attn_kernel, grid=(batch, nk),
        in_specs=[new_spec, cache_spec, cache_spec, new_spec, new_spec],
        out_specs=pl.BlockSpec((n_new, D_MODEL), lambda b, j: (b, 0)),
        out_shape=jax.ShapeDtypeStruct((batch * n_new, D_MODEL), BF16),
        scratch_shapes=[pltpu.VMEM((N_HEADS, n_new, 1), F32),
                        pltpu.VMEM((N_HEADS, n_new, HEAD_PAD), F32)],
        compiler_params=_params(2), name="sample_attention")(q, kc, vc, kn, vn)


def _lane_vec(parts):
    v = jnp.concatenate(parts)
    return jnp.pad(v, (0, LANES - v.shape[0])).reshape(1, LANES)


def _rope_tables(pos):
    inv = ROPE_BASE ** (-jnp.arange(HALF_ROPE, dtype=F32) / HALF_ROPE)
    ang = pos.astype(F32)[:, None] * inv[None, :]
    cos, sin = jnp.cos(ang), jnp.sin(ang)
    n = pos.shape[0]
    ones = jnp.ones((n, NOPE_DIM), F32)
    zeros_n = jnp.zeros((n, NOPE_DIM), F32)
    zeros_t = jnp.zeros((n, HEAD_PAD - QK_DIM), F32)
    cos_t = jnp.concatenate([ones, cos, cos, zeros_t], axis=1)
    sin_t = jnp.concatenate([zeros_n, -sin, sin, zeros_t], axis=1)
    return cos_t, sin_t


def _prep_mla(w_in, w_uq, w_ukv, q_nope_g, q_rope_g, k_nope_g, k_rope_g):
    zeros = lambda n: jnp.zeros((n,), F32)
    swap = lambda g: jnp.concatenate([g[HALF_ROPE:], g[:HALF_ROPE]])
    tail = HEAD_PAD - QK_DIM
    w_q, w_kv = w_in[:, :Q_LORA], w_in[:, Q_LORA:Q_LORA + KV_LORA]
    w_kr1 = w_in[:, Q_LORA + KV_LORA:Q_LORA + KV_LORA + HALF_ROPE]
    w_kr2 = w_in[:, Q_LORA + KV_LORA + HALF_ROPE:]
    zc = lambda n: jnp.zeros((D_MODEL, n), F32)
    win = jnp.concatenate([w_q, w_kv, zc(NOPE_DIM), w_kr1, w_kr2, zc(tail),
                           zc(NOPE_DIM), w_kr2, w_kr1, zc(tail)], axis=1).astype(BF16)
    gains = dict(krg=_lane_vec([zeros(NOPE_DIM), k_rope_g]),
                 krgs=_lane_vec([zeros(NOPE_DIM), swap(k_rope_g)]),
                 qhg=_lane_vec([q_nope_g, q_rope_g]) * (ATTN_SCALE * LOG2E),
                 qhgs=_lane_vec([zeros(NOPE_DIM), swap(q_rope_g)]) * (ATTN_SCALE * LOG2E))

    uq = w_uq.reshape(Q_LORA, N_HEADS, QK_DIM)
    uq_n, uq_1, uq_2 = uq[..., :NOPE_DIM], uq[..., NOPE_DIM:NOPE_DIM + HALF_ROPE], uq[..., NOPE_DIM + HALF_ROPE:]
    zq = lambda n: jnp.zeros((Q_LORA, N_HEADS, n), F32)
    main = jnp.concatenate([uq_n, uq_1, uq_2, zq(tail)], axis=-1).reshape(Q_LORA, Q_PAD)
    swapped = jnp.concatenate([zq(NOPE_DIM), uq_2, uq_1, zq(tail)], axis=-1).reshape(Q_LORA, Q_PAD)
    wuq = jnp.concatenate([main, swapped], axis=1).astype(BF16)

    ukv = w_ukv.reshape(KV_LORA, N_HEADS // 2, 2, NOPE_DIM + V_DIM)
    zk = jnp.zeros((KV_LORA, N_HEADS // 2, 2, HEAD_PAD - NOPE_DIM), F32)
    wk = jnp.concatenate([ukv[..., :NOPE_DIM], zk], axis=-1).reshape(KV_LORA, Q_PAD).astype(BF16)
    zv = jnp.zeros((KV_LORA, N_HEADS // 2, V_DIM), F32)
    v_even = jnp.concatenate([ukv[:, :, 0, NOPE_DIM:], zv], axis=-1)
    v_odd = jnp.concatenate([zv, ukv[:, :, 1, NOPE_DIM:]], axis=-1)
    wv = jnp.stack([v_even, v_odd], axis=2).reshape(KV_LORA, Q_PAD).astype(BF16)
    return win, wuq, wk, wv, _lane_vec([k_nope_g]), gains


def _gain_tables(cos_t, sin_t, gains):
    return (cos_t * gains["qhg"], sin_t * gains["qhgs"], cos_t * gains["krg"], sin_t * gains["krgs"])


def kernel(x_prompt, x_sample, cache_kv_latent, cache_k_rope, mix_norm_g, mlp_norm_g, sgu_w_in, sgu_norm_g, sgu_w_s, sgu_b_s, sgu_w_out, mla_w_in, mla_q_norm_g, mla_kv_norm_g, mla_w_uq, mla_w_ukv, mla_q_nope_g, mla_q_rope_g, mla_k_nope_g, mla_k_rope_g, mla_w_o, mlp_w_up, mlp_w_down):
    batch, seq, _ = x_prompt.shape
    dec_batch, n_new, _ = x_sample.shape
    past = cache_kv_latent.shape[2]
    depth = mix_norm_g.shape[0]
    rows_p, rows_s = batch * seq, dec_batch * n_new

    xp = x_prompt.reshape(rows_p, D_MODEL)
    xs = x_sample.reshape(rows_s, D_MODEL)
    cos_p, sin_p = _rope_tables(jnp.arange(seq))
    cos_s, sin_s = _rope_tables(past + jnp.arange(n_new))
    cos_s, sin_s = jnp.tile(cos_s, (dec_batch, 1)), jnp.tile(sin_s, (dec_batch, 1))
    mlp_wu, mlp_wd = mlp_w_up.astype(BF16), mlp_w_down.astype(BF16)
    sgu_win, sgu_wout = sgu_w_in.astype(BF16), sgu_w_out.astype(BF16)
    mla_wo = mla_w_o.astype(BF16)
    reps = SGU_CHUNK // n_new
    sgu_ws_s = jnp.tile(sgu_w_s[:, :, :n_new, :n_new], (1, 1, reps, reps))
    sgu_b_p = sgu_b_s[..., None]
    sgu_b_smp = jnp.tile(sgu_b_s[:, :, :n_new], (1, 1, reps))[..., None]

    lat_p, kr_p, lat_s, kr_s, sgu_v_s = [], [], [], [], []
    for i in range(depth):
        j = i // 2
        g_mix = mix_norm_g[i].reshape(1, D_MODEL)
        g_mlp = mlp_norm_g[i].reshape(1, D_MODEL)
        wu, wd = mlp_wu[i], mlp_wd[i]
        if i % 2 == 0:
            ng = sgu_norm_g[j].reshape(1, D_SGU)
            xp = _sgu_call(xp, g_mix, sgu_win[j], ng, sgu_w_s[j], sgu_b_p[j], sgu_wout[j],
                           block_diag=False, emit_v=False)
            xs, v_new = _sgu_call(xs, g_mix, sgu_win[j], ng, sgu_ws_s[j], sgu_b_smp[j], sgu_wout[j],
                                  block_diag=True, emit_v=True)
            sgu_v_s.append(v_new.reshape(dec_batch, n_new, D_SGU))
            xp = _mlp_call(xp, g_mlp, wu, wd)
            xs = _mlp_call(xs, g_mlp, wu, wd)
        else:
            win, wuq, wk, wv, kg, gains = _prep_mla(
                mla_w_in[j], mla_w_uq[j], mla_w_ukv[j], mla_q_nope_g[j], mla_q_rope_g[j],
                mla_k_nope_g[j], mla_k_rope_g[j])
            qg, kvg = mla_q_norm_g[j].reshape(1, -1), mla_kv_norm_g[j].reshape(1, -1)
            c, kr, krpad, q = _mla_proj_call(xp, g_mix, win, qg, kvg, wuq, _gain_tables(cos_p, sin_p, gains))
            k, v = _kv_expand_call(c, krpad, wk, wv, kg)
            attn = _prompt_attn_call(q, k, v, batch, seq)
            xp = _mlp_call(xp, g_mlp, wu, wd, attn=attn, wo=mla_wo[j])
            lat_p.append(c.reshape(batch, seq, KV_LORA))
            kr_p.append(kr.reshape(batch, seq, ROPE_DIM))
            c, kr, krpad, q = _mla_proj_call(xs, g_mix, win, qg, kvg, wuq, _gain_tables(cos_s, sin_s, gains))
            kn, vn = _kv_expand_call(c, krpad, wk, wv, kg)
            cache_c = cache_kv_latent[j].reshape(dec_batch * past, KV_LORA)
            cache_kr = jnp.pad(cache_k_rope[j].reshape(dec_batch * past, ROPE_DIM).astype(BF16),
                               ((0, 0), (NOPE_DIM, HEAD_PAD - QK_DIM)))
            kc, vc = _kv_expand_call(cache_c, cache_kr, wk, wv, kg)
            attn = _sample_attn_call(q, kc, vc, kn, vn, dec_batch, n_new, past)
            xs = _mlp_call(xs, g_mlp, wu, wd, attn=attn, wo=mla_wo[j])
            lat_s.append(c.reshape(dec_batch, n_new, KV_LORA))
            kr_s.append(kr.reshape(dec_batch, n_new, ROPE_DIM))
    return (xp.reshape(batch, seq, D_MODEL), xs.reshape(dec_batch, n_new, D_MODEL),
            jnp.stack(lat_p), jnp.stack(kr_p), jnp.stack(lat_s), jnp.stack(kr_s), jnp.stack(sgu_v_s))
```

```python
import functools
import math

import jax
import jax.numpy as jnp
import numpy as np
from jax import lax
from jax.experimental import pallas as pl
from jax.experimental.pallas import tpu as pltpu

F32 = jnp.float32
BF16 = jnp.bfloat16

D_MODEL = 1024
CHUNK = 64
SGU_CHUNK = 128
D_SGU = 2 * D_MODEL
SGU_GROUP_DIM = 128
SGU_GROUPS = D_SGU // SGU_GROUP_DIM
N_HEADS = 16
Q_LORA = 384
KV_LORA = 256
NOPE_DIM = 64
ROPE_DIM = 32
HALF_ROPE = ROPE_DIM // 2
V_DIM = 64
QK_DIM = NOPE_DIM + ROPE_DIM
ATTN_SCALE = 1.0 / math.sqrt(QK_DIM)
ROPE_BASE = 10000.0
D_FF = 4 * D_MODEL
EPS = 1e-6
NEG_INF = -1e30

LANES = 128
HEAD_PAD = LANES
FF_CHUNK = 1024
SGU_COL_CHUNK = 512
VMEM_LIMIT = 56 * 1024 * 1024


def _params(n_axes):
    return pltpu.CompilerParams(
        dimension_semantics=("arbitrary",) * n_axes, vmem_limit_bytes=VMEM_LIMIT)


def _const_spec(shape):
    nd = len(shape)
    return pl.BlockSpec(shape, lambda *_: (0,) * nd, pipeline_mode=pl.Buffered(1))


def _row_tile(rows, want):
    tm = min(rows, want)
    assert rows % tm == 0
    return tm


def _rms(xf, g, n):
    ss = jnp.sum(xf * xf, axis=-1, keepdims=True)
    return xf * lax.rsqrt(ss * (1.0 / n) + EPS) * g


def _gelu(z):
    return 0.5 * z * (1.0 + lax.erf(z * np.float32(math.sqrt(0.5))))


def _mlp_delta(x, g_ref, wu_ref, wd_ref):
    h = _rms(x, g_ref[...], D_MODEL).astype(BF16)
    acc = None
    for f in range(D_FF // FF_CHUNK):
        sl = slice(f * FF_CHUNK, (f + 1) * FF_CHUNK)
        a = jnp.dot(h, wu_ref[:, sl], preferred_element_type=F32)
        a = jnp.square(jnp.maximum(a, 0.0)).astype(BF16)
        d = jnp.dot(a, wd_ref[sl, :], preferred_element_type=F32)
        acc = d if acc is None else acc + d
    return acc


def _mlp_kernel(x_ref, g_ref, wu_ref, wd_ref, o_ref):
    x = x_ref[...]
    o_ref[...] = x + _mlp_delta(x, g_ref, wu_ref, wd_ref)


def _attn_out_mlp_kernel(x_ref, a_ref, wo_ref, g_ref, wu_ref, wd_ref, o_ref):
    x = x_ref[...] + jnp.dot(a_ref[...], wo_ref[...], preferred_element_type=F32)
    o_ref[...] = x + _mlp_delta(x, g_ref, wu_ref, wd_ref)


def _mlp_call(x, g, wu, wd, attn=None, wo=None, tm_want=512):
    rows = x.shape[0]
    tm = _row_tile(rows, tm_want)
    row_spec = pl.BlockSpec((tm, D_MODEL), lambda i: (i, 0))
    w_specs = [_const_spec((1, D_MODEL)), _const_spec((D_MODEL, D_FF)), _const_spec((D_FF, D_MODEL))]
    if attn is None:
        kern, ins, specs = _mlp_kernel, (x, g, wu, wd), [row_spec] + w_specs
        name = "channel_mlp"
    else:
        kern, ins = _attn_out_mlp_kernel, (x, attn, wo, g, wu, wd)
        specs = [row_spec, row_spec, _const_spec((D_MODEL, D_MODEL))] + w_specs
        name = "attn_out_channel_mlp"
    return pl.pallas_call(
        kern, grid=(rows // tm,), in_specs=specs, out_specs=row_spec,
        out_shape=jax.ShapeDtypeStruct((rows, D_MODEL), F32),
        compiler_params=_params(1), name=name)(*ins)


def _sgu_kernel(x_ref, g_ref, win_ref, ng_ref, ws_ref, b_ref, wout_ref, *rest, block_diag, emit_v):
    if emit_v:
        o_ref, v_ref, u_scr, v_scr, y_scr = rest
    else:
        o_ref, u_scr, v_scr, y_scr = rest
        v_ref = None
    tm = x_ref.shape[0]
    n_chunks = tm // SGU_CHUNK
    x = x_ref[...]
    h = _rms(x, g_ref[...], D_MODEL).astype(BF16)

    ss = jnp.zeros((tm, 1), F32)
    for k in range(D_SGU // SGU_COL_CHUNK):
        sl = slice(k * SGU_COL_CHUNK, (k + 1) * SGU_COL_CHUNK)
        u_scr[:, sl] = _gelu(jnp.dot(h, win_ref[:, sl], preferred_element_type=F32))
        slv = slice(D_SGU + k * SGU_COL_CHUNK, D_SGU + (k + 1) * SGU_COL_CHUNK)
        zv = _gelu(jnp.dot(h, win_ref[:, slv], preferred_element_type=F32))
        ss = ss + jnp.sum(zv * zv, axis=-1, keepdims=True)
        v_scr[:, sl] = zv
    r = lax.rsqrt(ss * (1.0 / D_SGU) + EPS)

    ri = lax.broadcasted_iota(jnp.int32, (SGU_CHUNK, SGU_CHUNK), 0) // CHUNK
    ci = lax.broadcasted_iota(jnp.int32, (SGU_CHUNK, SGU_CHUNK), 1) // CHUNK
    mask = (ri == ci) if block_diag else (ri >= ci)
    for grp in range(SGU_GROUPS):
        gsl = slice(grp * SGU_GROUP_DIM, (grp + 1) * SGU_GROUP_DIM)
        ws = jnp.where(mask, ws_ref[grp], 0.0).astype(BF16)
        pieces = []
        for c in range(n_chunks):
            rsl = slice(c * SGU_CHUNK, (c + 1) * SGU_CHUNK)
            vn = v_scr[rsl, gsl] * r[rsl] * ng_ref[:, gsl]
            if emit_v:
                v_ref[rsl, gsl] = vn
            pieces.append(vn.astype(BF16))
        rhs = pieces[0] if n_chunks == 1 else jnp.concatenate(pieces, axis=1)
        mixed = jnp.dot(ws, rhs, preferred_element_type=F32) + b_ref[grp]
        for c in range(n_chunks):
            rsl = slice(c * SGU_CHUNK, (c + 1) * SGU_CHUNK)
            y = u_scr[rsl, gsl] * mixed[:, c * SGU_CHUNK:(c + 1) * SGU_CHUNK]
            y_scr[rsl, gsl] = y.astype(BF16)
    o_ref[...] = x + jnp.dot(y_scr[...], wout_ref[...], preferred_element_type=F32)


def _sgu_call(x, g, win, ng, ws, b, wout, *, block_diag, emit_v, tm_want=512):
    rows = x.shape[0]
    tm = _row_tile(rows, tm_want)
    row_spec = pl.BlockSpec((tm, D_MODEL), lambda i: (i, 0))
    specs = [row_spec, _const_spec((1, D_MODEL)), _const_spec((D_MODEL, 2 * D_SGU)),
             _const_spec((1, D_SGU)), _const_spec((SGU_GROUPS, SGU_CHUNK, SGU_CHUNK)),
             _const_spec((SGU_GROUPS, SGU_CHUNK, 1)), _const_spec((D_SGU, D_MODEL))]
    out_shape = [jax.ShapeDtypeStruct((rows, D_MODEL), F32)]
    out_specs = [row_spec]
    if emit_v:
        out_shape.append(jax.ShapeDtypeStruct((rows, D_SGU), F32))
        out_specs.append(pl.BlockSpec((tm, D_SGU), lambda i: (i, 0)))
    res = pl.pallas_call(
        functools.partial(_sgu_kernel, block_diag=block_diag, emit_v=emit_v),
        grid=(rows // tm,), in_specs=specs, out_specs=out_specs, out_shape=out_shape,
        scratch_shapes=[pltpu.VMEM((tm, D_SGU), F32), pltpu.VMEM((tm, D_SGU), F32),
                        pltpu.VMEM((tm, D_SGU), BF16)],
        compiler_params=_params(1), name="sgu_mixer_v" if emit_v else "sgu_mixer")(
            x, g, win, ng, ws, b, wout)
    return res if emit_v else res[0]


W_IN_PAD = Q_LORA + KV_LORA + 2 * LANES
Q_PAD = N_HEADS * HEAD_PAD
LOG2E = math.log2(math.e)


def _ones_lane(hd):
    return V_DIM if hd % 2 == 0 else 0


def _mla_proj_kernel(x_ref, g_ref, win_ref, qg_ref, kvg_ref, wuq_ref, qcos_ref, qsin_ref,
                     kcos_ref, ksin_ref, ckv_ref, kr_ref, krpad_ref, q_ref):
    x = x_ref[...]
    h = _rms(x, g_ref[...], D_MODEL).astype(BF16)
    a = jnp.dot(h, win_ref[...], preferred_element_type=F32)
    c_q = _rms(a[:, :Q_LORA], qg_ref[...], Q_LORA)
    ckv_ref[...] = _rms(a[:, Q_LORA:Q_LORA + KV_LORA], kvg_ref[...], KV_LORA)

    kr = a[:, Q_LORA + KV_LORA:Q_LORA + KV_LORA + LANES]
    kr_sw = a[:, Q_LORA + KV_LORA + LANES:]
    r_kr = lax.rsqrt(jnp.sum(kr * kr, axis=-1, keepdims=True) * (1.0 / ROPE_DIM) + EPS)
    kr_rot = (kr * r_kr) * kcos_ref[...] + (kr_sw * r_kr) * ksin_ref[...]
    krpad_ref[...] = kr_rot.astype(BF16)
    kr_ref[...] = kr_rot[:, NOPE_DIM:NOPE_DIM + ROPE_DIM]

    qq = jnp.dot(c_q.astype(BF16), wuq_ref[...], preferred_element_type=F32)
    qcos = qcos_ref[...]
    qsin = qsin_ref[...]
    is_nope = lax.broadcasted_iota(jnp.int32, (1, HEAD_PAD), 1) < NOPE_DIM
    for hd in range(N_HEADS):
        blk = qq[:, hd * HEAD_PAD:(hd + 1) * HEAD_PAD]
        blk_sw = qq[:, Q_PAD + hd * HEAD_PAD:Q_PAD + (hd + 1) * HEAD_PAD]
        sq = blk * blk
        s_n = jnp.sum(jnp.where(is_nope, sq, 0.0), axis=-1, keepdims=True)
        s_r = jnp.sum(jnp.where(is_nope, 0.0, sq), axis=-1, keepdims=True)
        r_n = lax.rsqrt(s_n * (1.0 / NOPE_DIM) + EPS)
        r_r = lax.rsqrt(s_r * (1.0 / ROPE_DIM) + EPS)
        q_ref[hd] = ((blk * jnp.where(is_nope, r_n, r_r)) * qcos + (blk_sw * r_r) * qsin).astype(BF16)


def _mla_proj_call(x, g, win, qg, kvg, wuq, tables, tm_want=512):
    rows = x.shape[0]
    period = tables[0].shape[0]
    tm = _row_tile(min(rows, period), tm_want)
    n_per = period // tm
    row = lambda w: pl.BlockSpec((tm, w), lambda i: (i, 0))
    tab = pl.BlockSpec((tm, HEAD_PAD), lambda i: (i % n_per, 0))
    specs = [row(D_MODEL), _const_spec((1, D_MODEL)), _const_spec((D_MODEL, W_IN_PAD)),
             _const_spec((1, Q_LORA)), _const_spec((1, KV_LORA)), _const_spec((Q_LORA, 2 * Q_PAD)),
             tab, tab, tab, tab]
    out_shape = [jax.ShapeDtypeStruct((rows, KV_LORA), F32),
                 jax.ShapeDtypeStruct((rows, ROPE_DIM), F32),
                 jax.ShapeDtypeStruct((rows, LANES), BF16),
                 jax.ShapeDtypeStruct((N_HEADS, rows, HEAD_PAD), BF16)]
    out_specs = [row(KV_LORA), row(ROPE_DIM), row(LANES),
                 pl.BlockSpec((N_HEADS, tm, HEAD_PAD), lambda i: (0, i, 0))]
    return pl.pallas_call(
        _mla_proj_kernel, grid=(rows // tm,), in_specs=specs, out_specs=out_specs,
        out_shape=out_shape, compiler_params=_params(1), name="mla_project")(
            x, g, win, qg, kvg, wuq, *tables)


def _expand_keys(c, krp, wk_ref, kg_ref, k_dst):
    kk = jnp.dot(c, wk_ref[...], preferred_element_type=F32)
    for hd in range(N_HEADS):
        blk = kk[:, hd * HEAD_PAD:(hd + 1) * HEAD_PAD]
        r = lax.rsqrt(jnp.sum(blk * blk, axis=-1, keepdims=True) * (1.0 / NOPE_DIM) + EPS)
        k_dst[hd] = (blk * r * kg_ref[...] + krp).astype(BF16)


def _expand_values(c, wv_ref, v_dst):
    vv = jnp.dot(c, wv_ref[...], preferred_element_type=F32)
    lane = lax.broadcasted_iota(jnp.int32, (1, HEAD_PAD), 1)
    for hd in range(N_HEADS):
        vblk = vv[:, hd * HEAD_PAD:(hd + 1) * HEAD_PAD]
        v_dst[hd] = jnp.where(lane == _ones_lane(hd), 1.0, vblk).astype(BF16)


def _kv_expand_kernel(c_ref, krpad_ref, wk_ref, wv_ref, kg_ref, k_ref, v_ref, *, transposed_values):
    c = c_ref[...].astype(BF16)
    _expand_keys(c, krpad_ref[...].astype(F32), wk_ref, kg_ref, k_ref)
    if not transposed_values:
        _expand_values(c, wv_ref, v_ref)
        return
    vt = lax.dot_general(wv_ref[...], c, (((1,), (1,)), ((), ())), preferred_element_type=F32)
    row = lax.broadcasted_iota(jnp.int32, (HEAD_PAD, 1), 0)
    for hd in range(N_HEADS):
        blk = vt[hd * HEAD_PAD:(hd + 1) * HEAD_PAD, :]
        v_ref[hd, 0] = jnp.where(row == _ones_lane(hd), 1.0, blk).astype(BF16)


def _kv_expand_call(c, krpad, wk, wv, kg, *, transposed_values, tm_want=512):
    rows = c.shape[0]
    tm = _row_tile(rows, tm_want)
    row = lambda w: pl.BlockSpec((tm, w), lambda i: (i, 0))
    head = pl.BlockSpec((N_HEADS, tm, HEAD_PAD), lambda i: (0, i, 0))
    hs = jax.ShapeDtypeStruct((N_HEADS, rows, HEAD_PAD), BF16)
    if transposed_values:
        v_spec = pl.BlockSpec((N_HEADS, 1, HEAD_PAD, tm), lambda i: (0, i, 0, 0))
        v_shape = jax.ShapeDtypeStruct((N_HEADS, rows // tm, HEAD_PAD, tm), BF16)
    else:
        v_spec, v_shape = head, hs
    return pl.pallas_call(
        functools.partial(_kv_expand_kernel, transposed_values=transposed_values), grid=(rows // tm,),
        in_specs=[row(KV_LORA), row(LANES), _const_spec(wk.shape), _const_spec(wv.shape),
                  _const_spec((1, HEAD_PAD))],
        out_specs=[head, v_spec], out_shape=[hs, v_shape],
        compiler_params=_params(1), name="kv_expand_t" if transposed_values else "kv_expand")(
            c, krpad, wk, wv, kg)


def _scores(q, k):
    batch_dims = tuple(range(q.ndim - 2))
    return lax.dot_general(q, k, (((q.ndim - 1,), (k.ndim - 1,)), (batch_dims, batch_dims)),
                           preferred_element_type=F32)


def _online_update(s, v, m, acc):
    m_new = jnp.maximum(m, jnp.max(s, axis=-1, keepdims=True))
    alpha = jnp.exp2(m - m_new)
    p = jnp.exp2(s - m_new).astype(BF16)
    batch_dims = tuple(range(p.ndim - 2))
    pv = lax.dot_general(p, v, (((p.ndim - 1,), (v.ndim - 2,)), (batch_dims, batch_dims)),
                         preferred_element_type=F32)
    return m_new, alpha * acc + pv


def _pair_output(acc_even, acc_odd):
    l_even = acc_even[:, _ones_lane(0):_ones_lane(0) + 1]
    l_odd = acc_odd[:, _ones_lane(1):_ones_lane(1) + 1]
    lane = lax.broadcasted_iota(jnp.int32, (1, HEAD_PAD), 1)
    return jnp.where(lane < V_DIM, acc_even / l_even, acc_odd / l_odd).astype(BF16)


HEADS_PER_STEP = 4


def _prompt_attn_kernel(q_ref, k_ref, vt_ref, o_ref, *, tq, tk):
    qi = pl.program_id(1)
    n_full = (qi * tq) // tk
    key_chunk = lax.broadcasted_iota(jnp.int32, (tk, tq), 0) // CHUNK
    qry_chunk = lax.broadcasted_iota(jnp.int32, (tk, tq), 1) // CHUNK
    diag_mask = key_chunk <= qry_chunk
    row = lax.broadcasted_iota(jnp.int32, (HEAD_PAD, 1), 0)
    for group in range(N_HEADS // HEADS_PER_STEP):
        heads = tuple(range(group * HEADS_PER_STEP, (group + 1) * HEADS_PER_STEP))
        qs = [q_ref[hd] for hd in heads]

        def step(kb, carry, mask, qs=qs, heads=heads):
            ks = pl.ds(pl.multiple_of(kb * tk, tk), tk)
            ss = [_scores(k_ref[hd, ks, :], q) for q, hd in zip(qs, heads)]
            out = []
            for s, hd, (m, acc) in zip(ss, heads, carry):
                if mask is not None:
                    s = jnp.where(mask, s, NEG_INF)
                m_new = jnp.maximum(m, jnp.max(s, axis=0, keepdims=True))
                alpha = jnp.exp2(m - m_new)
                p = jnp.exp2(s - m_new).astype(BF16)
                out.append((m_new, alpha * acc + jnp.dot(vt_ref[hd, kb], p, preferred_element_type=F32)))
            return tuple(out)

        init = tuple((jnp.full((1, tq), NEG_INF, F32), jnp.zeros((HEAD_PAD, tq), F32)) for _ in heads)
        carry = lax.fori_loop(0, n_full, functools.partial(step, mask=None), init)
        carry = step(n_full, carry, diag_mask)
        for p in range(HEADS_PER_STEP // 2):
            (_, acc_e), (_, acc_o) = carry[2 * p], carry[2 * p + 1]
            l_e = acc_e[_ones_lane(0):_ones_lane(0) + 1, :]
            l_o = acc_o[_ones_lane(1):_ones_lane(1) + 1, :]
            o_t = jnp.where(row < V_DIM, acc_e / l_e, acc_o / l_o)
            pair = heads[2 * p] // 2
            o_ref[:, pair * LANES:(pair + 1) * LANES] = o_t.T.astype(BF16)


def _prompt_attn_call(q, k, vt, batch, seq, tq=512):
    tq = min(tq, seq)
    nq = seq // tq
    rows = batch * seq
    return pl.pallas_call(
        functools.partial(_prompt_attn_kernel, tq=tq, tk=tq),
        grid=(batch, nq),
        in_specs=[pl.BlockSpec((N_HEADS, tq, HEAD_PAD), lambda b, i: (0, b * nq + i, 0)),
                  pl.BlockSpec((N_HEADS, seq, HEAD_PAD), lambda b, i: (0, b, 0)),
                  pl.BlockSpec((N_HEADS, nq, HEAD_PAD, tq), lambda b, i: (0, b, 0, 0))],
        out_specs=pl.BlockSpec((tq, D_MODEL), lambda b, i: (b * nq + i, 0)),
        out_shape=jax.ShapeDtypeStruct((rows, D_MODEL), BF16),
        compiler_params=_params(2), name="prompt_attention")(q, k, vt)


def _sample_attn_kernel(q_ref, c_ref, kr_ref, wk_ref, wv_ref, kg_ref, kn_ref, vn_ref, o_ref,
                        k_scr, v_scr, m_scr, acc_scr):
    kb = pl.program_id(1)
    last = pl.num_programs(1) - 1

    @pl.when(kb == 0)
    def _():
        m_scr[...] = jnp.full(m_scr.shape, NEG_INF, F32)
        acc_scr[...] = jnp.zeros(acc_scr.shape, F32)

    place = (lax.broadcasted_iota(jnp.int32, (ROPE_DIM, HEAD_PAD), 0) + NOPE_DIM
             == lax.broadcasted_iota(jnp.int32, (ROPE_DIM, HEAD_PAD), 1)).astype(BF16)
    krp = jnp.dot(kr_ref[...].astype(BF16), place, preferred_element_type=F32)
    c = c_ref[...].astype(BF16)
    _expand_keys(c, krp, wk_ref, kg_ref, k_scr)
    _expand_values(c, wv_ref, v_scr)

    q = q_ref[...]
    m, acc = _online_update(_scores(q, k_scr[...]), v_scr[...], m_scr[...], acc_scr[...])
    m_scr[...] = m
    acc_scr[...] = acc

    @pl.when(kb == last)
    def _():
        _, acc_f = _online_update(_scores(q, kn_ref[...]), vn_ref[...], m, acc)
        for pair in range(N_HEADS // 2):
            o_ref[:, pair * LANES:(pair + 1) * LANES] = _pair_output(acc_f[2 * pair], acc_f[2 * pair + 1])


def _sample_attn_call(q, cache_c, cache_kr, wk, wv, kg, kn, vn, batch, n_new, past, tk=512):
    tk = min(tk, past)
    nk = past // tk
    new_spec = pl.BlockSpec((N_HEADS, n_new, HEAD_PAD), lambda b, j: (0, b, 0))
    cache = lambda w: pl.BlockSpec((tk, w), lambda b, j: (b * nk + j, 0))
    return pl.pallas_call(
        _sample_attn_kernel, grid=(batch, nk),
        in_specs=[new_spec, cache(KV_LORA), cache(ROPE_DIM), _const_spec((KV_LORA, Q_PAD)),
                  _const_spec((KV_LORA, Q_PAD)), _const_spec((1, HEAD_PAD)), new_spec, new_spec],
        out_specs=pl.BlockSpec((n_new, D_MODEL), lambda b, j: (b, 0)),
        out_shape=jax.ShapeDtypeStruct((batch * n_new, D_MODEL), BF16),
        scratch_shapes=[pltpu.VMEM((N_HEADS, tk, HEAD_PAD), BF16),
                        pltpu.VMEM((N_HEADS, tk, HEAD_PAD), BF16),
                        pltpu.VMEM((N_HEADS, n_new, 1), F32),
                        pltpu.VMEM((N_HEADS, n_new, HEAD_PAD), F32)],
        compiler_params=_params(2), name="sample_attention")(q, cache_c, cache_kr, wk, wv, kg, kn, vn)


def _lane_vec(parts):
    v = jnp.concatenate(parts)
    return jnp.pad(v, (0, LANES - v.shape[0])).reshape(1, LANES)


def _rope_tables(pos):
    inv = ROPE_BASE ** (-jnp.arange(HALF_ROPE, dtype=F32) / HALF_ROPE)
    ang = pos.astype(F32)[:, None] * inv[None, :]
    cos, sin = jnp.cos(ang), jnp.sin(ang)
    n = pos.shape[0]
    ones = jnp.ones((n, NOPE_DIM), F32)
    zeros_n = jnp.zeros((n, NOPE_DIM), F32)
    zeros_t = jnp.zeros((n, HEAD_PAD - QK_DIM), F32)
    cos_t = jnp.concatenate([ones, cos, cos, zeros_t], axis=1)
    sin_t = jnp.concatenate([zeros_n, -sin, sin, zeros_t], axis=1)
    return cos_t, sin_t


def _prep_mla(w_in, w_uq, w_ukv, q_nope_g, q_rope_g, k_nope_g, k_rope_g):
    zeros = lambda n: jnp.zeros((n,), F32)
    swap = lambda g: jnp.concatenate([g[HALF_ROPE:], g[:HALF_ROPE]])
    tail = HEAD_PAD - QK_DIM
    w_q, w_kv = w_in[:, :Q_LORA], w_in[:, Q_LORA:Q_LORA + KV_LORA]
    w_kr1 = w_in[:, Q_LORA + KV_LORA:Q_LORA + KV_LORA + HALF_ROPE]
    w_kr2 = w_in[:, Q_LORA + KV_LORA + HALF_ROPE:]
    zc = lambda n: jnp.zeros((D_MODEL, n), F32)
    win = jnp.concatenate([w_q, w_kv, zc(NOPE_DIM), w_kr1, w_kr2, zc(tail),
                           zc(NOPE_DIM), w_kr2, w_kr1, zc(tail)], axis=1).astype(BF16)
    gains = dict(krg=_lane_vec([zeros(NOPE_DIM), k_rope_g]),
                 krgs=_lane_vec([zeros(NOPE_DIM), swap(k_rope_g)]),
                 qhg=_lane_vec([q_nope_g, q_rope_g]) * (ATTN_SCALE * LOG2E),
                 qhgs=_lane_vec([zeros(NOPE_DIM), swap(q_rope_g)]) * (ATTN_SCALE * LOG2E))

    uq = w_uq.reshape(Q_LORA, N_HEADS, QK_DIM)
    uq_n, uq_1, uq_2 = uq[..., :NOPE_DIM], uq[..., NOPE_DIM:NOPE_DIM + HALF_ROPE], uq[..., NOPE_DIM + HALF_ROPE:]
    zq = lambda n: jnp.zeros((Q_LORA, N_HEADS, n), F32)
    main = jnp.concatenate([uq_n, uq_1, uq_2, zq(tail)], axis=-1).reshape(Q_LORA, Q_PAD)
    swapped = jnp.concatenate([zq(NOPE_DIM), uq_2, uq_1, zq(tail)], axis=-1).reshape(Q_LORA, Q_PAD)
    wuq = jnp.concatenate([main, swapped], axis=1).astype(BF16)

    ukv = w_ukv.reshape(KV_LORA, N_HEADS // 2, 2, NOPE_DIM + V_DIM)
    zk = jnp.zeros((KV_LORA, N_HEADS // 2, 2, HEAD_PAD - NOPE_DIM), F32)
    wk = jnp.concatenate([ukv[..., :NOPE_DIM], zk], axis=-1).reshape(KV_LORA, Q_PAD).astype(BF16)
    zv = jnp.zeros((KV_LORA, N_HEADS // 2, V_DIM), F32)
    v_even = jnp.concatenate([ukv[:, :, 0, NOPE_DIM:], zv], axis=-1)
    v_odd = jnp.concatenate([zv, ukv[:, :, 1, NOPE_DIM:]], axis=-1)
    wv = jnp.stack([v_even, v_odd], axis=2).reshape(KV_LORA, Q_PAD).astype(BF16)
    return win, wuq, wk, wv, _lane_vec([k_nope_g]), gains


def _gain_tables(cos_t, sin_t, gains):
    return (cos_t * gains["qhg"], sin_t * gains["qhgs"], cos_t * gains["krg"], sin_t * gains["krgs"])


def kernel(x_prompt, x_sample, cache_kv_latent, cache_k_rope, mix_norm_g, mlp_norm_g, sgu_w_in, sgu_norm_g, sgu_w_s, sgu_b_s, sgu_w_out, mla_w_in, mla_q_norm_g, mla_kv_norm_g, mla_w_uq, mla_w_ukv, mla_q_nope_g, mla_q_rope_g, mla_k_nope_g, mla_k_rope_g, mla_w_o, mlp_w_up, mlp_w_down):
    batch, seq, _ = x_prompt.shape
    dec_batch, n_new, _ = x_sample.shape
    past = cache_kv_latent.shape[2]
    depth = mix_norm_g.shape[0]
    rows_p, rows_s = batch * seq, dec_batch * n_new

    xp = x_prompt.reshape(rows_p, D_MODEL)
    xs = x_sample.reshape(rows_s, D_MODEL)
    cos_p, sin_p = _rope_tables(jnp.arange(seq))
    cos_s, sin_s = _rope_tables(past + jnp.arange(n_new))
    cos_s, sin_s = jnp.tile(cos_s, (dec_batch, 1)), jnp.tile(sin_s, (dec_batch, 1))
    mlp_wu, mlp_wd = mlp_w_up.astype(BF16), mlp_w_down.astype(BF16)
    sgu_win, sgu_wout = sgu_w_in.astype(BF16), sgu_w_out.astype(BF16)
    mla_wo = mla_w_o.astype(BF16)
    reps = SGU_CHUNK // n_new
    sgu_ws_s = jnp.tile(sgu_w_s[:, :, :n_new, :n_new], (1, 1, reps, reps))
    sgu_b_p = sgu_b_s[..., None]
    sgu_b_smp = jnp.tile(sgu_b_s[:, :, :n_new], (1, 1, reps))[..., None]

    lat_p, kr_p, lat_s, kr_s, sgu_v_s = [], [], [], [], []
    for i in range(depth):
        j = i // 2
        g_mix = mix_norm_g[i].reshape(1, D_MODEL)
        g_mlp = mlp_norm_g[i].reshape(1, D_MODEL)
        wu, wd = mlp_wu[i], mlp_wd[i]
        if i % 2 == 0:
            ng = sgu_norm_g[j].reshape(1, D_SGU)
            xp = _sgu_call(xp, g_mix, sgu_win[j], ng, sgu_w_s[j], sgu_b_p[j], sgu_wout[j],
                           block_diag=False, emit_v=False)
            xs, v_new = _sgu_call(xs, g_mix, sgu_win[j], ng, sgu_ws_s[j], sgu_b_smp[j], sgu_wout[j],
                                  block_diag=True, emit_v=True)
            sgu_v_s.append(v_new.reshape(dec_batch, n_new, D_SGU))
            xp = _mlp_call(xp, g_mlp, wu, wd)
            xs = _mlp_call(xs, g_mlp, wu, wd)
        else:
            win, wuq, wk, wv, kg, gains = _prep_mla(
                mla_w_in[j], mla_w_uq[j], mla_w_ukv[j], mla_q_nope_g[j], mla_q_rope_g[j],
                mla_k_nope_g[j], mla_k_rope_g[j])
            qg, kvg = mla_q_norm_g[j].reshape(1, -1), mla_kv_norm_g[j].reshape(1, -1)
            c, kr, krpad, q = _mla_proj_call(xp, g_mix, win, qg, kvg, wuq, _gain_tables(cos_p, sin_p, gains))
            k, vt = _kv_expand_call(c, krpad, wk, wv.T, kg, transposed_values=True)
            attn = _prompt_attn_call(q, k, vt, batch, seq)
            xp = _mlp_call(xp, g_mlp, wu, wd, attn=attn, wo=mla_wo[j])
            lat_p.append(c.reshape(batch, seq, KV_LORA))
            kr_p.append(kr.reshape(batch, seq, ROPE_DIM))
            c, kr, krpad, q = _mla_proj_call(xs, g_mix, win, qg, kvg, wuq, _gain_tables(cos_s, sin_s, gains))
            kn, vn = _kv_expand_call(c, krpad, wk, wv, kg, transposed_values=False)
            cache_c = cache_kv_latent[j].reshape(dec_batch * past, KV_LORA)
            cache_kr = cache_k_rope[j].reshape(dec_batch * past, ROPE_DIM)
            attn = _sample_attn_call(q, cache_c, cache_kr, wk, wv, kg, kn, vn, dec_batch, n_new, past)
            xs = _mlp_call(xs, g_mlp, wu, wd, attn=attn, wo=mla_wo[j])
            lat_s.append(c.reshape(dec_batch, n_new, KV_LORA))
            kr_s.append(kr.reshape(dec_batch, n_new, ROPE_DIM))
    return (xp.reshape(batch, seq, D_MODEL), xs.reshape(dec_batch, n_new, D_MODEL),
            jnp.stack(lat_p), jnp.stack(kr_p), jnp.stack(lat_s), jnp.stack(kr_s), jnp.stack(sgu_v_s))
```

```python
import functools
import math

import jax
import jax.numpy as jnp
import numpy as np
from jax import lax
from jax.experimental import pallas as pl
from jax.experimental.pallas import tpu as pltpu

F32 = jnp.float32
BF16 = jnp.bfloat16

D_MODEL = 1024
CHUNK = 64
SGU_CHUNK = 128
D_SGU = 2 * D_MODEL
SGU_GROUP_DIM = 128
SGU_GROUPS = D_SGU // SGU_GROUP_DIM
N_HEADS = 16
Q_LORA = 384
KV_LORA = 256
NOPE_DIM = 64
ROPE_DIM = 32
HALF_ROPE = ROPE_DIM // 2
V_DIM = 64
QK_DIM = NOPE_DIM + ROPE_DIM
ATTN_SCALE = 1.0 / math.sqrt(QK_DIM)
ROPE_BASE = 10000.0
D_FF = 4 * D_MODEL
EPS = 1e-6
NEG_INF = -1e30

LANES = 128
HEAD_PAD = LANES
FF_CHUNK = 1024
SGU_COL_CHUNK = 512
VMEM_LIMIT = 56 * 1024 * 1024


def _params(n_axes):
    return pltpu.CompilerParams(
        dimension_semantics=("arbitrary",) * n_axes, vmem_limit_bytes=VMEM_LIMIT)


def _const_spec(shape):
    nd = len(shape)
    return pl.BlockSpec(shape, lambda *_: (0,) * nd, pipeline_mode=pl.Buffered(1))


def _row_tile(rows, want):
    tm = min(rows, want)
    assert rows % tm == 0
    return tm


def _rms(xf, g, n):
    ss = jnp.sum(xf * xf, axis=-1, keepdims=True)
    return xf * lax.rsqrt(ss * (1.0 / n) + EPS) * g


def _gelu(z):
    return 0.5 * z * (1.0 + lax.erf(z * np.float32(math.sqrt(0.5))))


def _mlp_delta(x, g_ref, wu_ref, wd_ref):
    h = _rms(x, g_ref[...], D_MODEL).astype(BF16)
    acc = None
    for f in range(D_FF // FF_CHUNK):
        sl = slice(f * FF_CHUNK, (f + 1) * FF_CHUNK)
        a = jnp.dot(h, wu_ref[:, sl], preferred_element_type=F32)
        a = jnp.square(jnp.maximum(a, 0.0)).astype(BF16)
        d = jnp.dot(a, wd_ref[sl, :], preferred_element_type=F32)
        acc = d if acc is None else acc + d
    return acc


def _mlp_kernel(x_ref, g_ref, wu_ref, wd_ref, o_ref):
    x = x_ref[...]
    o_ref[...] = x + _mlp_delta(x, g_ref, wu_ref, wd_ref)


def _attn_out_mlp_kernel(x_ref, a_ref, wo_ref, g_ref, wu_ref, wd_ref, o_ref):
    x = x_ref[...] + jnp.dot(a_ref[...], wo_ref[...], preferred_element_type=F32)
    o_ref[...] = x + _mlp_delta(x, g_ref, wu_ref, wd_ref)


def _mlp_call(x, g, wu, wd, attn=None, wo=None, tm_want=512):
    rows = x.shape[0]
    tm = _row_tile(rows, tm_want)
    row_spec = pl.BlockSpec((tm, D_MODEL), lambda i: (i, 0))
    w_specs = [_const_spec((1, D_MODEL)), _const_spec((D_MODEL, D_FF)), _const_spec((D_FF, D_MODEL))]
    if attn is None:
        kern, ins, specs = _mlp_kernel, (x, g, wu, wd), [row_spec] + w_specs
        name = "channel_mlp"
    else:
        kern, ins = _attn_out_mlp_kernel, (x, attn, wo, g, wu, wd)
        specs = [row_spec, row_spec, _const_spec((D_MODEL, D_MODEL))] + w_specs
        name = "attn_out_channel_mlp"
    return pl.pallas_call(
        kern, grid=(rows // tm,), in_specs=specs, out_specs=row_spec,
        out_shape=jax.ShapeDtypeStruct((rows, D_MODEL), F32),
        compiler_params=_params(1), name=name)(*ins)


def _sgu_kernel(x_ref, g_ref, win_ref, ng_ref, ws_ref, b_ref, wout_ref, *rest, block_diag, emit_v):
    if emit_v:
        o_ref, v_ref, u_scr, v_scr, y_scr = rest
    else:
        o_ref, u_scr, v_scr, y_scr = rest
        v_ref = None
    tm = x_ref.shape[0]
    n_chunks = tm // SGU_CHUNK
    x = x_ref[...]
    h = _rms(x, g_ref[...], D_MODEL).astype(BF16)

    ss = jnp.zeros((tm, 1), F32)
    for k in range(D_SGU // SGU_COL_CHUNK):
        sl = slice(k * SGU_COL_CHUNK, (k + 1) * SGU_COL_CHUNK)
        u_scr[:, sl] = _gelu(jnp.dot(h, win_ref[:, sl], preferred_element_type=F32))
        slv = slice(D_SGU + k * SGU_COL_CHUNK, D_SGU + (k + 1) * SGU_COL_CHUNK)
        zv = _gelu(jnp.dot(h, win_ref[:, slv], preferred_element_type=F32))
        ss = ss + jnp.sum(zv * zv, axis=-1, keepdims=True)
        v_scr[:, sl] = zv
    r = lax.rsqrt(ss * (1.0 / D_SGU) + EPS)

    ri = lax.broadcasted_iota(jnp.int32, (SGU_CHUNK, SGU_CHUNK), 0) // CHUNK
    ci = lax.broadcasted_iota(jnp.int32, (SGU_CHUNK, SGU_CHUNK), 1) // CHUNK
    mask = (ri == ci) if block_diag else (ri >= ci)
    for grp in range(SGU_GROUPS):
        gsl = slice(grp * SGU_GROUP_DIM, (grp + 1) * SGU_GROUP_DIM)
        ws = jnp.where(mask, ws_ref[grp], 0.0).astype(BF16)
        pieces = []
        for c in range(n_chunks):
            rsl = slice(c * SGU_CHUNK, (c + 1) * SGU_CHUNK)
            vn = v_scr[rsl, gsl] * r[rsl] * ng_ref[:, gsl]
            if emit_v:
                v_ref[rsl, gsl] = vn
            pieces.append(vn.astype(BF16))
        rhs = pieces[0] if n_chunks == 1 else jnp.concatenate(pieces, axis=1)
        mixed = jnp.dot(ws, rhs, preferred_element_type=F32) + b_ref[grp]
        for c in range(n_chunks):
            rsl = slice(c * SGU_CHUNK, (c + 1) * SGU_CHUNK)
            y = u_scr[rsl, gsl] * mixed[:, c * SGU_CHUNK:(c + 1) * SGU_CHUNK]
            y_scr[rsl, gsl] = y.astype(BF16)
    o_ref[...] = x + jnp.dot(y_scr[...], wout_ref[...], preferred_element_type=F32)


def _sgu_call(x, g, win, ng, ws, b, wout, *, block_diag, emit_v, tm_want=512):
    rows = x.shape[0]
    tm = _row_tile(rows, tm_want)
    row_spec = pl.BlockSpec((tm, D_MODEL), lambda i: (i, 0))
    specs = [row_spec, _const_spec((1, D_MODEL)), _const_spec((D_MODEL, 2 * D_SGU)),
             _const_spec((1, D_SGU)), _const_spec((SGU_GROUPS, SGU_CHUNK, SGU_CHUNK)),
             _const_spec((SGU_GROUPS, SGU_CHUNK, 1)), _const_spec((D_SGU, D_MODEL))]
    out_shape = [jax.ShapeDtypeStruct((rows, D_MODEL), F32)]
    out_specs = [row_spec]
    if emit_v:
        out_shape.append(jax.ShapeDtypeStruct((rows, D_SGU), F32))
        out_specs.append(pl.BlockSpec((tm, D_SGU), lambda i: (i, 0)))
    res = pl.pallas_call(
        functools.partial(_sgu_kernel, block_diag=block_diag, emit_v=emit_v),
        grid=(rows // tm,), in_specs=specs, out_specs=out_specs, out_shape=out_shape,
        scratch_shapes=[pltpu.VMEM((tm, D_SGU), F32), pltpu.VMEM((tm, D_SGU), F32),
                        pltpu.VMEM((tm, D_SGU), BF16)],
        compiler_params=_params(1), name="sgu_mixer_v" if emit_v else "sgu_mixer")(
            x, g, win, ng, ws, b, wout)
    return res if emit_v else res[0]


ROPE_LANE0 = NOPE_DIM
ROPE_TAIL = HEAD_PAD - ROPE_LANE0 - ROPE_DIM
W_IN_PAD = Q_LORA + KV_LORA + 2 * LANES
Q_PAD = N_HEADS * HEAD_PAD
LOG2E = math.log2(math.e)


def _ones_lane(hd):
    return V_DIM if hd % 2 == 0 else 0


def _mla_proj_kernel(x_ref, g_ref, win_ref, qg_ref, kvg_ref, wuq_ref, qcos_ref, qsin_ref,
                     kcos_ref, ksin_ref, *rest):
    ckv_ref, kr_ref, krpad_ref, q_ref = rest[-4:]
    x = x_ref[...]
    h = _rms(x, g_ref[...], D_MODEL).astype(BF16)
    a = jnp.dot(h, win_ref[...], preferred_element_type=F32)
    c_q = _rms(a[:, :Q_LORA], qg_ref[...], Q_LORA)
    ckv_ref[...] = _rms(a[:, Q_LORA:Q_LORA + KV_LORA], kvg_ref[...], KV_LORA)

    kr = a[:, Q_LORA + KV_LORA:Q_LORA + KV_LORA + LANES]
    kr_sw = a[:, Q_LORA + KV_LORA + LANES:]
    r_kr = lax.rsqrt(jnp.sum(kr * kr, axis=-1, keepdims=True) * (1.0 / ROPE_DIM) + EPS)
    kr_rot = (kr * r_kr) * kcos_ref[...] + (kr_sw * r_kr) * ksin_ref[...]
    krpad_ref[...] = kr_rot.astype(BF16)
    kr_ref[...] = kr_rot[:, ROPE_LANE0:ROPE_LANE0 + ROPE_DIM]

    qq = jnp.dot(c_q.astype(BF16), wuq_ref[...], preferred_element_type=F32)
    qcos = qcos_ref[...]
    qsin = qsin_ref[...]
    is_nope = lax.broadcasted_iota(jnp.int32, (1, HEAD_PAD), 1) < NOPE_DIM
    for hd in range(N_HEADS):
        blk = qq[:, hd * HEAD_PAD:(hd + 1) * HEAD_PAD]
        blk_sw = qq[:, Q_PAD + hd * HEAD_PAD:Q_PAD + (hd + 1) * HEAD_PAD]
        sq = blk * blk
        s_n = jnp.sum(jnp.where(is_nope, sq, 0.0), axis=-1, keepdims=True)
        s_r = jnp.sum(jnp.where(is_nope, 0.0, sq), axis=-1, keepdims=True)
        r_n = lax.rsqrt(s_n * (1.0 / NOPE_DIM) + EPS)
        r_r = lax.rsqrt(s_r * (1.0 / ROPE_DIM) + EPS)
        q_ref[hd] = ((blk * jnp.where(is_nope, r_n, r_r)) * qcos + (blk_sw * r_r) * qsin).astype(BF16)


def _mla_proj_call(x, g, win, qg, kvg, wuq, tables, layer, n_layers, prev=None, tm_want=512):
    rows = x.shape[0]
    period = tables[0].shape[0]
    tm = _row_tile(min(rows, period), tm_want)
    n_per = period // tm
    row = lambda w: pl.BlockSpec((tm, w), lambda i: (i, 0))
    slot = lambda w: pl.BlockSpec((None, tm, w), lambda i: (layer, i, 0))
    tab = pl.BlockSpec((tm, HEAD_PAD), lambda i: (i % n_per, 0))
    specs = [row(D_MODEL), _const_spec((1, D_MODEL)), _const_spec((D_MODEL, W_IN_PAD)),
             _const_spec((1, Q_LORA)), _const_spec((1, KV_LORA)), _const_spec((Q_LORA, 2 * Q_PAD)),
             tab, tab, tab, tab]
    ins = [x, g, win, qg, kvg, wuq, *tables]
    aliases = {}
    if prev is not None:
        aliases = {len(ins): 0, len(ins) + 1: 1}
        ins += list(prev)
        specs += [pl.BlockSpec(memory_space=pl.ANY)] * 2
    out_shape = [jax.ShapeDtypeStruct((n_layers, rows, KV_LORA), F32),
                 jax.ShapeDtypeStruct((n_layers, rows, ROPE_DIM), F32),
                 jax.ShapeDtypeStruct((rows, LANES), BF16),
                 jax.ShapeDtypeStruct((N_HEADS, rows, HEAD_PAD), BF16)]
    out_specs = [slot(KV_LORA), slot(ROPE_DIM), row(LANES),
                 pl.BlockSpec((N_HEADS, tm, HEAD_PAD), lambda i: (0, i, 0))]
    return pl.pallas_call(
        _mla_proj_kernel, grid=(rows // tm,), in_specs=specs, out_specs=out_specs,
        out_shape=out_shape, input_output_aliases=aliases,
        compiler_params=_params(1), name="mla_project")(*ins)


def _expand_keys(c, krp, wk_ref, kg_ref, k_dst):
    kk = jnp.dot(c, wk_ref[...], preferred_element_type=F32)
    for hd in range(N_HEADS):
        blk = kk[:, hd * HEAD_PAD:(hd + 1) * HEAD_PAD]
        r = lax.rsqrt(jnp.sum(blk * blk, axis=-1, keepdims=True) * (1.0 / NOPE_DIM) + EPS)
        k_dst[hd] = (blk * r * kg_ref[...] + krp).astype(BF16)


def _expand_values(c, wv_ref, v_dst):
    vv = jnp.dot(c, wv_ref[...], preferred_element_type=F32)
    lane = lax.broadcasted_iota(jnp.int32, (1, HEAD_PAD), 1)
    for hd in range(N_HEADS):
        vblk = vv[:, hd * HEAD_PAD:(hd + 1) * HEAD_PAD]
        v_dst[hd] = jnp.where(lane == _ones_lane(hd), 1.0, vblk).astype(BF16)


def _kv_expand_kernel(c_ref, krpad_ref, wk_ref, wv_ref, kg_ref, k_ref, v_ref, *, transposed_values):
    c = c_ref[...].astype(BF16)
    _expand_keys(c, krpad_ref[...].astype(F32), wk_ref, kg_ref, k_ref)
    if not transposed_values:
        _expand_values(c, wv_ref, v_ref)
        return
    vt = lax.dot_general(wv_ref[...], c, (((1,), (1,)), ((), ())), preferred_element_type=F32)
    row = lax.broadcasted_iota(jnp.int32, (HEAD_PAD, 1), 0)
    for hd in range(N_HEADS):
        blk = vt[hd * HEAD_PAD:(hd + 1) * HEAD_PAD, :]
        v_ref[hd, 0] = jnp.where(row == _ones_lane(hd), 1.0, blk).astype(BF16)


def _kv_expand_call(c_stack, layer, krpad, wk, wv, kg, *, transposed_values, tm_want=512):
    rows = c_stack.shape[1]
    tm = _row_tile(rows, tm_want)
    row = lambda w: pl.BlockSpec((tm, w), lambda i: (i, 0))
    c_spec = pl.BlockSpec((None, tm, KV_LORA), lambda i: (layer, i, 0))
    head = pl.BlockSpec((N_HEADS, tm, HEAD_PAD), lambda i: (0, i, 0))
    hs = jax.ShapeDtypeStruct((N_HEADS, rows, HEAD_PAD), BF16)
    if transposed_values:
        v_spec = pl.BlockSpec((N_HEADS, 1, HEAD_PAD, tm), lambda i: (0, i, 0, 0))
        v_shape = jax.ShapeDtypeStruct((N_HEADS, rows // tm, HEAD_PAD, tm), BF16)
    else:
        v_spec, v_shape = head, hs
    return pl.pallas_call(
        functools.partial(_kv_expand_kernel, transposed_values=transposed_values), grid=(rows // tm,),
        in_specs=[c_spec, row(LANES), _const_spec(wk.shape), _const_spec(wv.shape),
                  _const_spec((1, HEAD_PAD))],
        out_specs=[head, v_spec], out_shape=[hs, v_shape],
        compiler_params=_params(1), name="kv_expand_t" if transposed_values else "kv_expand")(
            c_stack, krpad, wk, wv, kg)


def _scores(q, k):
    batch_dims = tuple(range(q.ndim - 2))
    return lax.dot_general(q, k, (((q.ndim - 1,), (k.ndim - 1,)), (batch_dims, batch_dims)),
                           preferred_element_type=F32)


def _online_update(s, v, m, acc):
    m_new = jnp.maximum(m, jnp.max(s, axis=-1, keepdims=True))
    alpha = jnp.exp2(m - m_new)
    p = jnp.exp2(s - m_new).astype(BF16)
    batch_dims = tuple(range(p.ndim - 2))
    pv = lax.dot_general(p, v, (((p.ndim - 1,), (v.ndim - 2,)), (batch_dims, batch_dims)),
                         preferred_element_type=F32)
    return m_new, alpha * acc + pv


def _pair_output(acc_even, acc_odd):
    l_even = acc_even[:, _ones_lane(0):_ones_lane(0) + 1]
    l_odd = acc_odd[:, _ones_lane(1):_ones_lane(1) + 1]
    lane = lax.broadcasted_iota(jnp.int32, (1, HEAD_PAD), 1)
    return jnp.where(lane < V_DIM, acc_even / l_even, acc_odd / l_odd).astype(BF16)


HEADS_PER_STEP = 4
DIAG_STRIP = 256


def _prompt_attn_kernel(q_ref, k_ref, vt_ref, o_ref, *, tq):
    seq = q_ref.shape[1]
    row = lax.broadcasted_iota(jnp.int32, (HEAD_PAD, 1), 0)
    heads = range(HEADS_PER_STEP)

    def update(s, vt, m, acc):
        m_new = jnp.maximum(m, jnp.max(s, axis=0, keepdims=True))
        alpha = jnp.exp2(m - m_new)
        p = jnp.exp2(s - m_new).astype(BF16)
        return m_new, alpha * acc + jnp.dot(vt, p, preferred_element_type=F32)

    for qi in range(seq // tq):
        q0 = qi * tq
        carry = [(jnp.full((1, tq), NEG_INF, F32), jnp.zeros((HEAD_PAD, tq), F32)) for _ in heads]
        for kb in range(qi):
            ss = [_scores(k_ref[hd, kb * tq:(kb + 1) * tq, :], q_ref[hd, q0:q0 + tq, :]) for hd in heads]
            carry = [update(s, vt_ref[hd, kb], m, acc) for s, hd, (m, acc) in zip(ss, heads, carry)]
        for j in range(tq // DIAG_STRIP):
            c0 = j * DIAG_STRIP
            n_keys = c0 + DIAG_STRIP
            key_chunk = lax.broadcasted_iota(jnp.int32, (n_keys, DIAG_STRIP), 0) // CHUNK
            qry_chunk = (lax.broadcasted_iota(jnp.int32, (n_keys, DIAG_STRIP), 1) + c0) // CHUNK
            mask = key_chunk <= qry_chunk
            ss = [_scores(k_ref[hd, q0:q0 + n_keys, :], q_ref[hd, q0 + c0:q0 + n_keys, :]) for hd in heads]
            accs = [update(jnp.where(mask, s, NEG_INF), vt_ref[hd, qi, :, :n_keys],
                           m[:, c0:c0 + DIAG_STRIP], acc[:, c0:c0 + DIAG_STRIP])[1]
                    for s, hd, (m, acc) in zip(ss, heads, carry)]
            for p in range(HEADS_PER_STEP // 2):
                acc_e, acc_o = accs[2 * p], accs[2 * p + 1]
                l_e = acc_e[_ones_lane(0):_ones_lane(0) + 1, :]
                l_o = acc_o[_ones_lane(1):_ones_lane(1) + 1, :]
                o_t = jnp.where(row < V_DIM, acc_e / l_e, acc_o / l_o)
                o_ref[q0 + c0:q0 + n_keys, p * LANES:(p + 1) * LANES] = o_t.T.astype(BF16)


def _prompt_attn_call(q, k, vt, batch, seq, tq=512):
    tq = min(tq, seq)
    nq = seq // tq
    rows = batch * seq
    assert vt.shape == (N_HEADS, batch * nq, HEAD_PAD, tq)
    g = HEADS_PER_STEP
    return pl.pallas_call(
        functools.partial(_prompt_attn_kernel, tq=tq),
        grid=(batch, N_HEADS // g),
        in_specs=[pl.BlockSpec((g, seq, HEAD_PAD), lambda b, i: (i, b, 0)),
                  pl.BlockSpec((g, seq, HEAD_PAD), lambda b, i: (i, b, 0)),
                  pl.BlockSpec((g, nq, HEAD_PAD, tq), lambda b, i: (i, b, 0, 0))],
        out_specs=pl.BlockSpec((seq, g // 2 * LANES), lambda b, i: (b, i)),
        out_shape=jax.ShapeDtypeStruct((rows, D_MODEL), BF16),
        compiler_params=_params(2), name="prompt_attention")(q, k, vt)


def _sample_attn_kernel(q_ref, c_ref, kr_ref, wk_ref, wv_ref, kg_ref, kn_ref, vn_ref, o_ref,
                        k_scr, v_scr, m_scr, acc_scr):
    kb = pl.program_id(1)
    last = pl.num_programs(1) - 1

    @pl.when(kb == 0)
    def _():
        m_scr[...] = jnp.full(m_scr.shape, NEG_INF, F32)
        acc_scr[...] = jnp.zeros(acc_scr.shape, F32)

    place = (lax.broadcasted_iota(jnp.int32, (ROPE_DIM, HEAD_PAD), 0) + ROPE_LANE0
             == lax.broadcasted_iota(jnp.int32, (ROPE_DIM, HEAD_PAD), 1)).astype(BF16)
    krp = jnp.dot(kr_ref[...].astype(BF16), place, preferred_element_type=F32)
    c = c_ref[...].astype(BF16)
    _expand_keys(c, krp, wk_ref, kg_ref, k_scr)
    _expand_values(c, wv_ref, v_scr)

    q = q_ref[...]
    m, acc = _online_update(_scores(q, k_scr[...]), v_scr[...], m_scr[...], acc_scr[...])
    m_scr[...] = m
    acc_scr[...] = acc

    @pl.when(kb == last)
    def _():
        _, acc_f = _online_update(_scores(q, kn_ref[...]), vn_ref[...], m, acc)
        for pair in range(N_HEADS // 2):
            o_ref[:, pair * LANES:(pair + 1) * LANES] = _pair_output(acc_f[2 * pair], acc_f[2 * pair + 1])


def _sample_attn_call(q, cache_c, cache_kr, wk, wv, kg, kn, vn, batch, n_new, past, tk=512):
    tk = min(tk, past)
    nk = past // tk
    new_spec = pl.BlockSpec((N_HEADS, n_new, HEAD_PAD), lambda b, j: (0, b, 0))
    cache = lambda w: pl.BlockSpec((tk, w), lambda b, j: (b * nk + j, 0))
    return pl.pallas_call(
        _sample_attn_kernel, grid=(batch, nk),
        in_specs=[new_spec, cache(KV_LORA), cache(ROPE_DIM), _const_spec((KV_LORA, Q_PAD)),
                  _const_spec((KV_LORA, Q_PAD)), _const_spec((1, HEAD_PAD)), new_spec, new_spec],
        out_specs=pl.BlockSpec((n_new, D_MODEL), lambda b, j: (b, 0)),
        out_shape=jax.ShapeDtypeStruct((batch * n_new, D_MODEL), BF16),
        scratch_shapes=[pltpu.VMEM((N_HEADS, tk, HEAD_PAD), BF16),
                        pltpu.VMEM((N_HEADS, tk, HEAD_PAD), BF16),
                        pltpu.VMEM((N_HEADS, n_new, 1), F32),
                        pltpu.VMEM((N_HEADS, n_new, HEAD_PAD), F32)],
        compiler_params=_params(2), name="sample_attention")(q, cache_c, cache_kr, wk, wv, kg, kn, vn)


def _lane_vec(parts):
    v = jnp.concatenate(parts)
    return jnp.pad(v, (0, LANES - v.shape[0])).reshape(1, LANES)


def _rope_tables(pos):
    inv = ROPE_BASE ** (-jnp.arange(HALF_ROPE, dtype=F32) / HALF_ROPE)
    ang = pos.astype(F32)[:, None] * inv[None, :]
    cos, sin = jnp.cos(ang), jnp.sin(ang)
    n = pos.shape[0]
    zeros = lambda w: jnp.zeros((n, w), F32)
    cos_t = jnp.concatenate([jnp.ones((n, NOPE_DIM), F32), cos, cos, zeros(ROPE_TAIL)], axis=1)
    sin_t = jnp.concatenate([zeros(ROPE_LANE0), -sin, sin, zeros(ROPE_TAIL)], axis=1)
    return cos_t, sin_t


def _prep_mla(w_in, w_uq, w_ukv, q_nope_g, q_rope_g, k_nope_g, k_rope_g):
    zeros = lambda n: jnp.zeros((n,), F32)
    swap = lambda g: jnp.concatenate([g[HALF_ROPE:], g[:HALF_ROPE]])
    w_q, w_kv = w_in[:, :Q_LORA], w_in[:, Q_LORA:Q_LORA + KV_LORA]
    w_kr1 = w_in[:, Q_LORA + KV_LORA:Q_LORA + KV_LORA + HALF_ROPE]
    w_kr2 = w_in[:, Q_LORA + KV_LORA + HALF_ROPE:]
    zc = lambda n: jnp.zeros((D_MODEL, n), F32)
    win = jnp.concatenate([w_q, w_kv, zc(ROPE_LANE0), w_kr1, w_kr2, zc(ROPE_TAIL),
                           zc(ROPE_LANE0), w_kr2, w_kr1, zc(ROPE_TAIL)], axis=1).astype(BF16)
    gains = dict(krg=_lane_vec([zeros(ROPE_LANE0), k_rope_g]),
                 krgs=_lane_vec([zeros(ROPE_LANE0), swap(k_rope_g)]),
                 qhg=_lane_vec([q_nope_g, q_rope_g]) * (ATTN_SCALE * LOG2E),
                 qhgs=_lane_vec([zeros(ROPE_LANE0), swap(q_rope_g)]) * (ATTN_SCALE * LOG2E))

    uq = w_uq.reshape(Q_LORA, N_HEADS, QK_DIM)
    uq_n, uq_1, uq_2 = uq[..., :NOPE_DIM], uq[..., NOPE_DIM:NOPE_DIM + HALF_ROPE], uq[..., NOPE_DIM + HALF_ROPE:]
    zq = lambda n: jnp.zeros((Q_LORA, N_HEADS, n), F32)
    main = jnp.concatenate([uq_n, uq_1, uq_2, zq(ROPE_TAIL)], axis=-1).reshape(Q_LORA, Q_PAD)
    swapped = jnp.concatenate([zq(ROPE_LANE0), uq_2, uq_1, zq(ROPE_TAIL)], axis=-1).reshape(Q_LORA, Q_PAD)
    wuq = jnp.concatenate([main, swapped], axis=1).astype(BF16)

    ukv = w_ukv.reshape(KV_LORA, N_HEADS // 2, 2, NOPE_DIM + V_DIM)
    zk = jnp.zeros((KV_LORA, N_HEADS // 2, 2, HEAD_PAD - NOPE_DIM), F32)
    wk = jnp.concatenate([ukv[..., :NOPE_DIM], zk], axis=-1).reshape(KV_LORA, Q_PAD).astype(BF16)
    zv = jnp.zeros((KV_LORA, N_HEADS // 2, V_DIM), F32)
    v_even = jnp.concatenate([ukv[:, :, 0, NOPE_DIM:], zv], axis=-1)
    v_odd = jnp.concatenate([zv, ukv[:, :, 1, NOPE_DIM:]], axis=-1)
    wv = jnp.stack([v_even, v_odd], axis=2).reshape(KV_LORA, Q_PAD).astype(BF16)
    return win, wuq, wk, wv, _lane_vec([k_nope_g]), gains


def _gain_tables(cos_t, sin_t, gains):
    return (cos_t * gains["qhg"], sin_t * gains["qhgs"], cos_t * gains["krg"], sin_t * gains["krgs"])


def kernel(x_prompt, x_sample, cache_kv_latent, cache_k_rope, mix_norm_g, mlp_norm_g, sgu_w_in, sgu_norm_g, sgu_w_s, sgu_b_s, sgu_w_out, mla_w_in, mla_q_norm_g, mla_kv_norm_g, mla_w_uq, mla_w_ukv, mla_q_nope_g, mla_q_rope_g, mla_k_nope_g, mla_k_rope_g, mla_w_o, mlp_w_up, mlp_w_down):
    batch, seq, _ = x_prompt.shape
    dec_batch, n_new, _ = x_sample.shape
    past = cache_kv_latent.shape[2]
    depth = mix_norm_g.shape[0]
    rows_p, rows_s = batch * seq, dec_batch * n_new

    xp = x_prompt.reshape(rows_p, D_MODEL)
    xs = x_sample.reshape(rows_s, D_MODEL)
    cos_p, sin_p = _rope_tables(jnp.arange(seq))
    cos_s, sin_s = _rope_tables(past + jnp.arange(n_new))
    cos_s, sin_s = jnp.tile(cos_s, (dec_batch, 1)), jnp.tile(sin_s, (dec_batch, 1))
    mlp_wu, mlp_wd = mlp_w_up.astype(BF16), mlp_w_down.astype(BF16)
    sgu_win, sgu_wout = sgu_w_in.astype(BF16), sgu_w_out.astype(BF16)
    mla_wo = mla_w_o.astype(BF16)
    reps = SGU_CHUNK // n_new
    sgu_ws_s = jnp.tile(sgu_w_s[:, :, :n_new, :n_new], (1, 1, reps, reps))
    sgu_b_p = sgu_b_s[..., None]
    sgu_b_smp = jnp.tile(sgu_b_s[:, :, :n_new], (1, 1, reps))[..., None]

    n_mla = mla_w_in.shape[0]
    cache_p = cache_s = None
    sgu_v_s = []
    for i in range(depth):
        j = i // 2
        g_mix = mix_norm_g[i].reshape(1, D_MODEL)
        g_mlp = mlp_norm_g[i].reshape(1, D_MODEL)
        wu, wd = mlp_wu[i], mlp_wd[i]
        if i % 2 == 0:
            ng = sgu_norm_g[j].reshape(1, D_SGU)
            xp = _sgu_call(xp, g_mix, sgu_win[j], ng, sgu_w_s[j], sgu_b_p[j], sgu_wout[j],
                           block_diag=False, emit_v=False)
            xs, v_new = _sgu_call(xs, g_mix, sgu_win[j], ng, sgu_ws_s[j], sgu_b_smp[j], sgu_wout[j],
                                  block_diag=True, emit_v=True)
            sgu_v_s.append(v_new.reshape(dec_batch, n_new, D_SGU))
            xp = _mlp_call(xp, g_mlp, wu, wd)
            xs = _mlp_call(xs, g_mlp, wu, wd)
        else:
            win, wuq, wk, wv, kg, gains = _prep_mla(
                mla_w_in[j], mla_w_uq[j], mla_w_ukv[j], mla_q_nope_g[j], mla_q_rope_g[j],
                mla_k_nope_g[j], mla_k_rope_g[j])
            qg, kvg = mla_q_norm_g[j].reshape(1, -1), mla_kv_norm_g[j].reshape(1, -1)
            *cache_p, krpad, q = _mla_proj_call(xp, g_mix, win, qg, kvg, wuq,
                                                _gain_tables(cos_p, sin_p, gains), j, n_mla, cache_p)
            k, vt = _kv_expand_call(cache_p[0], j, krpad, wk, wv.T, kg, transposed_values=True)
            attn = _prompt_attn_call(q, k, vt, batch, seq)
            xp = _mlp_call(xp, g_mlp, wu, wd, attn=attn, wo=mla_wo[j])
            *cache_s, krpad, q = _mla_proj_call(xs, g_mix, win, qg, kvg, wuq,
                                                _gain_tables(cos_s, sin_s, gains), j, n_mla, cache_s)
            kn, vn = _kv_expand_call(cache_s[0], j, krpad, wk, wv, kg, transposed_values=False)
            cache_c = cache_kv_latent[j].reshape(dec_batch * past, KV_LORA)
            cache_kr = cache_k_rope[j].reshape(dec_batch * past, ROPE_DIM)
            attn = _sample_attn_call(q, cache_c, cache_kr, wk, wv, kg, kn, vn, dec_batch, n_new, past)
            xs = _mlp_call(xs, g_mlp, wu, wd, attn=attn, wo=mla_wo[j])
    return (xp.reshape(batch, seq, D_MODEL), xs.reshape(dec_batch, n_new, D_MODEL),
            cache_p[0].reshape(n_mla, batch, seq, KV_LORA), cache_p[1].reshape(n_mla, batch, seq, ROPE_DIM),
            cache_s[0].reshape(n_mla, dec_batch, n_new, KV_LORA),
            cache_s[1].reshape(n_mla, dec_batch, n_new, ROPE_DIM), jnp.stack(sgu_v_s))
```

```python
import functools
import math

import jax
import jax.numpy as jnp
import numpy as np
from jax import lax
from jax.experimental import pallas as pl
from jax.experimental.pallas import tpu as pltpu

F32 = jnp.float32
BF16 = jnp.bfloat16

D_MODEL = 1024
CHUNK = 64
SGU_CHUNK = 128
D_SGU = 2 * D_MODEL
SGU_GROUP_DIM = 128
SGU_GROUPS = D_SGU // SGU_GROUP_DIM
N_HEADS = 16
Q_LORA = 384
KV_LORA = 256
NOPE_DIM = 64
ROPE_DIM = 32
HALF_ROPE = ROPE_DIM // 2
V_DIM = 64
QK_DIM = NOPE_DIM + ROPE_DIM
ATTN_SCALE = 1.0 / math.sqrt(QK_DIM)
ROPE_BASE = 10000.0
D_FF = 4 * D_MODEL
EPS = 1e-6
NEG_INF = -1e30

LANES = 128
HEAD_PAD = LANES
FF_CHUNK = 1024
SGU_COL_CHUNK = 512
VMEM_LIMIT = 56 * 1024 * 1024
ATTN_BLOCK = 512


def _params(n_axes):
    return pltpu.CompilerParams(
        dimension_semantics=("arbitrary",) * n_axes, vmem_limit_bytes=VMEM_LIMIT)


def _const_spec(shape):
    nd = len(shape)
    return pl.BlockSpec(shape, lambda *_: (0,) * nd, pipeline_mode=pl.Buffered(1))


def _row_tile(rows, want):
    tm = min(rows, want)
    assert rows % tm == 0
    return tm


def _rms(xf, g, n):
    ss = jnp.sum(xf * xf, axis=-1, keepdims=True)
    return xf * lax.rsqrt(ss * (1.0 / n) + EPS) * g


def _gelu(z):
    return 0.5 * z * (1.0 + lax.erf(z * np.float32(math.sqrt(0.5))))


def _mlp_delta(x, g_ref, wu_ref, wd_ref):
    h = _rms(x, g_ref[...], D_MODEL).astype(BF16)
    acc = None
    for f in range(D_FF // FF_CHUNK):
        sl = slice(f * FF_CHUNK, (f + 1) * FF_CHUNK)
        a = jnp.dot(h, wu_ref[:, sl], preferred_element_type=F32)
        a = jnp.square(jnp.maximum(a, 0.0)).astype(BF16)
        d = jnp.dot(a, wd_ref[sl, :], preferred_element_type=F32)
        acc = d if acc is None else acc + d
    return acc


def _mlp_kernel(x_ref, g_ref, wu_ref, wd_ref, o_ref):
    x = x_ref[...]
    o_ref[...] = x + _mlp_delta(x, g_ref, wu_ref, wd_ref)


def _attn_out_mlp_kernel(x_ref, a_ref, wo_ref, g_ref, wu_ref, wd_ref, o_ref):
    x = x_ref[...] + jnp.dot(a_ref[...], wo_ref[...], preferred_element_type=F32)
    o_ref[...] = x + _mlp_delta(x, g_ref, wu_ref, wd_ref)


def _mlp_call(x, g, wu, wd, attn=None, wo=None, tm_want=512):
    rows = x.shape[0]
    tm = _row_tile(rows, tm_want)
    row_spec = pl.BlockSpec((tm, D_MODEL), lambda i: (i, 0))
    w_specs = [_const_spec((1, D_MODEL)), _const_spec((D_MODEL, D_FF)), _const_spec((D_FF, D_MODEL))]
    if attn is None:
        kern, ins, specs = _mlp_kernel, (x, g, wu, wd), [row_spec] + w_specs
        name = "channel_mlp"
    else:
        kern, ins = _attn_out_mlp_kernel, (x, attn, wo, g, wu, wd)
        specs = [row_spec, row_spec, _const_spec((D_MODEL, D_MODEL))] + w_specs
        name = "attn_out_channel_mlp"
    return pl.pallas_call(
        kern, grid=(rows // tm,), in_specs=specs, out_specs=row_spec,
        out_shape=jax.ShapeDtypeStruct((rows, D_MODEL), F32),
        compiler_params=_params(1), name=name)(*ins)


def _sgu_kernel(x_ref, g_ref, win_ref, ng_ref, ws_ref, b_ref, wout_ref, *rest, block_diag, emit_v):
    if emit_v:
        o_ref, v_ref, u_scr, v_scr, y_scr = rest
    else:
        o_ref, u_scr, v_scr, y_scr = rest
        v_ref = None
    tm = x_ref.shape[0]
    n_chunks = tm // SGU_CHUNK
    x = x_ref[...]
    h = _rms(x, g_ref[...], D_MODEL).astype(BF16)

    ss = jnp.zeros((tm, 1), F32)
    for k in range(D_SGU // SGU_COL_CHUNK):
        sl = slice(k * SGU_COL_CHUNK, (k + 1) * SGU_COL_CHUNK)
        u_scr[:, sl] = _gelu(jnp.dot(h, win_ref[:, sl], preferred_element_type=F32))
        slv = slice(D_SGU + k * SGU_COL_CHUNK, D_SGU + (k + 1) * SGU_COL_CHUNK)
        zv = _gelu(jnp.dot(h, win_ref[:, slv], preferred_element_type=F32))
        ss = ss + jnp.sum(zv * zv, axis=-1, keepdims=True)
        v_scr[:, sl] = zv
    r = lax.rsqrt(ss * (1.0 / D_SGU) + EPS)

    ri = lax.broadcasted_iota(jnp.int32, (SGU_CHUNK, SGU_CHUNK), 0) // CHUNK
    ci = lax.broadcasted_iota(jnp.int32, (SGU_CHUNK, SGU_CHUNK), 1) // CHUNK
    mask = (ri == ci) if block_diag else (ri >= ci)
    for grp in range(SGU_GROUPS):
        gsl = slice(grp * SGU_GROUP_DIM, (grp + 1) * SGU_GROUP_DIM)
        ws = jnp.where(mask, ws_ref[grp], 0.0).astype(BF16)
        pieces = []
        for c in range(n_chunks):
            rsl = slice(c * SGU_CHUNK, (c + 1) * SGU_CHUNK)
            vn = v_scr[rsl, gsl] * r[rsl] * ng_ref[:, gsl]
            if emit_v:
                v_ref[rsl, gsl] = vn
            pieces.append(vn.astype(BF16))
        rhs = pieces[0] if n_chunks == 1 else jnp.concatenate(pieces, axis=1)
        mixed = jnp.dot(ws, rhs, preferred_element_type=F32) + b_ref[grp]
        for c in range(n_chunks):
            rsl = slice(c * SGU_CHUNK, (c + 1) * SGU_CHUNK)
            y = u_scr[rsl, gsl] * mixed[:, c * SGU_CHUNK:(c + 1) * SGU_CHUNK]
            y_scr[rsl, gsl] = y.astype(BF16)
    o_ref[...] = x + jnp.dot(y_scr[...], wout_ref[...], preferred_element_type=F32)


def _sgu_call(x, g, win, ng, ws, b, wout, *, block_diag, emit_v, tm_want=512):
    rows = x.shape[0]
    tm = _row_tile(rows, tm_want)
    row_spec = pl.BlockSpec((tm, D_MODEL), lambda i: (i, 0))
    specs = [row_spec, _const_spec((1, D_MODEL)), _const_spec((D_MODEL, 2 * D_SGU)),
             _const_spec((1, D_SGU)), _const_spec((SGU_GROUPS, SGU_CHUNK, SGU_CHUNK)),
             _const_spec((SGU_GROUPS, SGU_CHUNK, 1)), _const_spec((D_SGU, D_MODEL))]
    out_shape = [jax.ShapeDtypeStruct((rows, D_MODEL), F32)]
    out_specs = [row_spec]
    if emit_v:
        out_shape.append(jax.ShapeDtypeStruct((rows, D_SGU), F32))
        out_specs.append(pl.BlockSpec((tm, D_SGU), lambda i: (i, 0)))
    res = pl.pallas_call(
        functools.partial(_sgu_kernel, block_diag=block_diag, emit_v=emit_v),
        grid=(rows // tm,), in_specs=specs, out_specs=out_specs, out_shape=out_shape,
        scratch_shapes=[pltpu.VMEM((tm, D_SGU), F32), pltpu.VMEM((tm, D_SGU), F32),
                        pltpu.VMEM((tm, D_SGU), BF16)],
        compiler_params=_params(1), name="sgu_mixer_v" if emit_v else "sgu_mixer")(
            x, g, win, ng, ws, b, wout)
    return res if emit_v else res[0]


ROPE_LANE0 = NOPE_DIM
ROPE_TAIL = HEAD_PAD - ROPE_LANE0 - ROPE_DIM
W_IN_PAD = Q_LORA + KV_LORA + 2 * LANES
Q_PAD = N_HEADS * HEAD_PAD
LOG2E = math.log2(math.e)


def _ones_lane(hd):
    return V_DIM if hd % 2 == 0 else 0


def _mla_proj_kernel(x_ref, g_ref, win_ref, qg_ref, kvg_ref, kcos_ref, ksin_ref, wuq_ref,
                     qtab_a_ref, qtab_b_ref, *rest, q_transposed):
    ckv_ref, kr_ref, krpad_ref, q_ref = rest[-4:]
    x = x_ref[...]
    h = _rms(x, g_ref[...], D_MODEL).astype(BF16)
    a = jnp.dot(h, win_ref[...], preferred_element_type=F32)
    c_q = _rms(a[:, :Q_LORA], qg_ref[...], Q_LORA).astype(BF16)
    ckv_ref[...] = _rms(a[:, Q_LORA:Q_LORA + KV_LORA], kvg_ref[...], KV_LORA)

    kr = a[:, Q_LORA + KV_LORA:Q_LORA + KV_LORA + LANES]
    kr_sw = a[:, Q_LORA + KV_LORA + LANES:]
    r_kr = lax.rsqrt(jnp.sum(kr * kr, axis=-1, keepdims=True) * (1.0 / ROPE_DIM) + EPS)
    kr_rot = (kr * r_kr) * kcos_ref[...] + (kr_sw * r_kr) * ksin_ref[...]
    krpad_ref[...] = kr_rot.astype(BF16)
    kr_ref[...] = kr_rot[:, ROPE_LANE0:ROPE_LANE0 + ROPE_DIM]

    if q_transposed:
        q_t = lax.dot_general(wuq_ref[...], c_q, (((1,), (1,)), ((), ())), preferred_element_type=F32)
        g_nope = qtab_a_ref[...]
        rot = qtab_b_ref[...]
        t_a, t_b, t_c, t_d = (rot[i * HALF_ROPE:(i + 1) * HALF_ROPE] for i in range(4))
        pad = jnp.zeros((ROPE_TAIL, q_t.shape[1]), F32)
        for hd in range(N_HEADS):
            base = hd * HEAD_PAD
            nope = q_t[base:base + NOPE_DIM]
            x1 = q_t[base + NOPE_DIM:base + NOPE_DIM + HALF_ROPE]
            x2 = q_t[base + NOPE_DIM + HALF_ROPE:base + QK_DIM]
            s_n = jnp.sum(nope * nope, axis=0, keepdims=True)
            s_r = jnp.sum(x1 * x1 + x2 * x2, axis=0, keepdims=True)
            r_n = lax.rsqrt(s_n * (1.0 / NOPE_DIM) + EPS)
            r_r = lax.rsqrt(s_r * (1.0 / ROPE_DIM) + EPS)
            x1, x2 = x1 * r_r, x2 * r_r
            q_ref[hd] = jnp.concatenate(
                [nope * r_n * g_nope, x1 * t_a - x2 * t_b, x2 * t_c + x1 * t_d, pad], axis=0).astype(BF16)
        return

    qq = jnp.dot(c_q, wuq_ref[...], preferred_element_type=F32)
    qcos = qtab_a_ref[...]
    qsin = qtab_b_ref[...]
    is_nope = lax.broadcasted_iota(jnp.int32, (1, HEAD_PAD), 1) < NOPE_DIM
    for hd in range(N_HEADS):
        blk = qq[:, hd * HEAD_PAD:(hd + 1) * HEAD_PAD]
        blk_sw = qq[:, Q_PAD + hd * HEAD_PAD:Q_PAD + (hd + 1) * HEAD_PAD]
        sq = blk * blk
        s_n = jnp.sum(jnp.where(is_nope, sq, 0.0), axis=-1, keepdims=True)
        s_r = jnp.sum(jnp.where(is_nope, 0.0, sq), axis=-1, keepdims=True)
        r_n = lax.rsqrt(s_n * (1.0 / NOPE_DIM) + EPS)
        r_r = lax.rsqrt(s_r * (1.0 / ROPE_DIM) + EPS)
        q_ref[hd] = ((blk * jnp.where(is_nope, r_n, r_r)) * qcos + (blk_sw * r_r) * qsin).astype(BF16)


def _mla_proj_call(x, g, win, qg, kvg, k_tables, wuq, q_tables, layer, n_layers, prev=None, *,
                   q_transposed, tm_want=512):
    rows = x.shape[0]
    period = k_tables[0].shape[0]
    tm = _row_tile(min(rows, period), tm_want)
    n_per = period // tm
    row = lambda w: pl.BlockSpec((tm, w), lambda i: (i, 0))
    slot = lambda w: pl.BlockSpec((None, tm, w), lambda i: (layer, i, 0))
    tab = pl.BlockSpec((tm, HEAD_PAD), lambda i: (i % n_per, 0))
    if q_transposed:
        q_specs = [_const_spec((NOPE_DIM, tm)), pl.BlockSpec((2 * ROPE_DIM, tm), lambda i: (0, i % n_per))]
        q_shape = jax.ShapeDtypeStruct((N_HEADS, HEAD_PAD, rows), BF16)
        q_out = pl.BlockSpec((N_HEADS, HEAD_PAD, tm), lambda i: (0, 0, i))
    else:
        q_specs = [tab, tab]
        q_shape = jax.ShapeDtypeStruct((N_HEADS, rows, HEAD_PAD), BF16)
        q_out = pl.BlockSpec((N_HEADS, tm, HEAD_PAD), lambda i: (0, i, 0))
    specs = [row(D_MODEL), _const_spec((1, D_MODEL)), _const_spec((D_MODEL, W_IN_PAD)),
             _const_spec((1, Q_LORA)), _const_spec((1, KV_LORA)), tab, tab,
             _const_spec(wuq.shape)] + q_specs
    ins = [x, g, win, qg, kvg, *k_tables, wuq, *q_tables]
    aliases = {}
    if prev is not None:
        aliases = {len(ins): 0, len(ins) + 1: 1}
        ins += list(prev)
        specs += [pl.BlockSpec(memory_space=pl.ANY)] * 2
    out_shape = [jax.ShapeDtypeStruct((n_layers, rows, KV_LORA), F32),
                 jax.ShapeDtypeStruct((n_layers, rows, ROPE_DIM), F32),
                 jax.ShapeDtypeStruct((rows, LANES), BF16), q_shape]
    out_specs = [slot(KV_LORA), slot(ROPE_DIM), row(LANES), q_out]
    return pl.pallas_call(
        functools.partial(_mla_proj_kernel, q_transposed=q_transposed), grid=(rows // tm,),
        in_specs=specs, out_specs=out_specs, out_shape=out_shape, input_output_aliases=aliases,
        compiler_params=_params(1), name="mla_project_t" if q_transposed else "mla_project")(*ins)


def _expand_keys(c, krp, wk_ref, kg_ref, k_dst):
    kk = jnp.dot(c, wk_ref[...], preferred_element_type=F32)
    for hd in range(N_HEADS):
        blk = kk[:, hd * HEAD_PAD:(hd + 1) * HEAD_PAD]
        r = lax.rsqrt(jnp.sum(blk * blk, axis=-1, keepdims=True) * (1.0 / NOPE_DIM) + EPS)
        k_dst[hd] = (blk * r * kg_ref[...] + krp).astype(BF16)


def _expand_values(c, wv_ref, v_dst):
    vv = jnp.dot(c, wv_ref[...], preferred_element_type=F32)
    lane = lax.broadcasted_iota(jnp.int32, (1, HEAD_PAD), 1)
    for hd in range(N_HEADS):
        vblk = vv[:, hd * HEAD_PAD:(hd + 1) * HEAD_PAD]
        v_dst[hd] = jnp.where(lane == _ones_lane(hd), 1.0, vblk).astype(BF16)


def _kv_expand_kernel(c_ref, krpad_ref, wk_ref, wv_ref, kg_ref, k_ref, v_ref, *, transposed_values):
    c = c_ref[...].astype(BF16)
    _expand_keys(c, krpad_ref[...].astype(F32), wk_ref, kg_ref, k_ref)
    if not transposed_values:
        _expand_values(c, wv_ref, v_ref)
        return
    vt = lax.dot_general(wv_ref[...], c, (((1,), (1,)), ((), ())), preferred_element_type=F32)
    row = lax.broadcasted_iota(jnp.int32, (HEAD_PAD, 1), 0)
    for hd in range(N_HEADS):
        blk = vt[hd * HEAD_PAD:(hd + 1) * HEAD_PAD, :]
        v_ref[hd, 0] = jnp.where(row == _ones_lane(hd), 1.0, blk).astype(BF16)


def _kv_expand_call(c_stack, layer, krpad, wk, wv, kg, *, transposed_values, tm_want=512):
    rows = c_stack.shape[1]
    tm = _row_tile(rows, tm_want)
    row = lambda w: pl.BlockSpec((tm, w), lambda i: (i, 0))
    c_spec = pl.BlockSpec((None, tm, KV_LORA), lambda i: (layer, i, 0))
    head = pl.BlockSpec((N_HEADS, tm, HEAD_PAD), lambda i: (0, i, 0))
    hs = jax.ShapeDtypeStruct((N_HEADS, rows, HEAD_PAD), BF16)
    if transposed_values:
        v_spec = pl.BlockSpec((N_HEADS, 1, HEAD_PAD, tm), lambda i: (0, i, 0, 0))
        v_shape = jax.ShapeDtypeStruct((N_HEADS, rows // tm, HEAD_PAD, tm), BF16)
    else:
        v_spec, v_shape = head, hs
    return pl.pallas_call(
        functools.partial(_kv_expand_kernel, transposed_values=transposed_values), grid=(rows // tm,),
        in_specs=[c_spec, row(LANES), _const_spec(wk.shape), _const_spec(wv.shape),
                  _const_spec((1, HEAD_PAD))],
        out_specs=[head, v_spec], out_shape=[hs, v_shape],
        compiler_params=_params(1), name="kv_expand_t" if transposed_values else "kv_expand")(
            c_stack, krpad, wk, wv, kg)


def _scores(q, k):
    batch_dims = tuple(range(q.ndim - 2))
    return lax.dot_general(q, k, (((q.ndim - 1,), (k.ndim - 1,)), (batch_dims, batch_dims)),
                           preferred_element_type=F32)


def _online_update(s, v, m, acc):
    m_new = jnp.maximum(m, jnp.max(s, axis=-1, keepdims=True))
    alpha = jnp.exp2(m - m_new)
    p = jnp.exp2(s - m_new).astype(BF16)
    batch_dims = tuple(range(p.ndim - 2))
    pv = lax.dot_general(p, v, (((p.ndim - 1,), (v.ndim - 2,)), (batch_dims, batch_dims)),
                         preferred_element_type=F32)
    return m_new, alpha * acc + pv


def _pair_output(acc_even, acc_odd):
    l_even = acc_even[:, _ones_lane(0):_ones_lane(0) + 1]
    l_odd = acc_odd[:, _ones_lane(1):_ones_lane(1) + 1]
    lane = lax.broadcasted_iota(jnp.int32, (1, HEAD_PAD), 1)
    return jnp.where(lane < V_DIM, acc_even / l_even, acc_odd / l_odd).astype(BF16)


HEADS_PER_STEP = 4
DIAG_STRIP = 256


def _prompt_attn_kernel(qt_ref, k_ref, vt_ref, o_ref, *, tq):
    seq = k_ref.shape[1]
    row = lax.broadcasted_iota(jnp.int32, (HEAD_PAD, 1), 0)
    heads = range(HEADS_PER_STEP)

    def update(s, vt, m, acc):
        m_new = jnp.maximum(m, jnp.max(s, axis=0, keepdims=True))
        alpha = jnp.exp2(m - m_new)
        p = jnp.exp2(s - m_new).astype(BF16)
        return m_new, alpha * acc + jnp.dot(vt, p, preferred_element_type=F32)

    for qi in range(seq // tq):
        q0 = qi * tq
        carry = [(jnp.full((1, tq), NEG_INF, F32), jnp.zeros((HEAD_PAD, tq), F32)) for _ in heads]
        for kb in range(qi):
            ss = [jnp.dot(k_ref[hd, kb * tq:(kb + 1) * tq, :], qt_ref[hd, :, q0:q0 + tq],
                          preferred_element_type=F32) for hd in heads]
            carry = [update(s, vt_ref[hd, kb], m, acc) for s, hd, (m, acc) in zip(ss, heads, carry)]
        for j in range(tq // DIAG_STRIP):
            c0 = j * DIAG_STRIP
            n_keys = c0 + DIAG_STRIP
            key_chunk = lax.broadcasted_iota(jnp.int32, (n_keys, DIAG_STRIP), 0) // CHUNK
            qry_chunk = (lax.broadcasted_iota(jnp.int32, (n_keys, DIAG_STRIP), 1) + c0) // CHUNK
            mask = key_chunk <= qry_chunk
            ss = [jnp.dot(k_ref[hd, q0:q0 + n_keys, :], qt_ref[hd, :, q0 + c0:q0 + n_keys],
                          preferred_element_type=F32) for hd in heads]
            accs = [update(jnp.where(mask, s, NEG_INF), vt_ref[hd, qi, :, :n_keys],
                           m[:, c0:c0 + DIAG_STRIP], acc[:, c0:c0 + DIAG_STRIP])[1]
                    for s, hd, (m, acc) in zip(ss, heads, carry)]
            for p in range(HEADS_PER_STEP // 2):
                acc_e, acc_o = accs[2 * p], accs[2 * p + 1]
                l_e = acc_e[_ones_lane(0):_ones_lane(0) + 1, :]
                l_o = acc_o[_ones_lane(1):_ones_lane(1) + 1, :]
                o_t = jnp.where(row < V_DIM, acc_e / l_e, acc_o / l_o)
                o_ref[q0 + c0:q0 + n_keys, p * LANES:(p + 1) * LANES] = o_t.T.astype(BF16)


def _prompt_attn_call(q_t, k, vt, batch, seq, tq=ATTN_BLOCK):
    tq = min(tq, seq)
    nq = seq // tq
    rows = batch * seq
    assert vt.shape == (N_HEADS, batch * nq, HEAD_PAD, tq)
    g = HEADS_PER_STEP
    return pl.pallas_call(
        functools.partial(_prompt_attn_kernel, tq=tq),
        grid=(batch, N_HEADS // g),
        in_specs=[pl.BlockSpec((g, HEAD_PAD, seq), lambda b, i: (i, 0, b)),
                  pl.BlockSpec((g, seq, HEAD_PAD), lambda b, i: (i, b, 0)),
                  pl.BlockSpec((g, nq, HEAD_PAD, tq), lambda b, i: (i, b, 0, 0))],
        out_specs=pl.BlockSpec((seq, g // 2 * LANES), lambda b, i: (b, i)),
        out_shape=jax.ShapeDtypeStruct((rows, D_MODEL), BF16),
        compiler_params=_params(2), name="prompt_attention")(q_t, k, vt)


def _sample_attn_kernel(q_ref, c_ref, kr_ref, wk_ref, wv_ref, kg_ref, kn_ref, vn_ref, o_ref,
                        k_scr, v_scr, m_scr, acc_scr):
    kb = pl.program_id(1)
    last = pl.num_programs(1) - 1

    @pl.when(kb == 0)
    def _():
        m_scr[...] = jnp.full(m_scr.shape, NEG_INF, F32)
        acc_scr[...] = jnp.zeros(acc_scr.shape, F32)

    place = (lax.broadcasted_iota(jnp.int32, (ROPE_DIM, HEAD_PAD), 0) + ROPE_LANE0
             == lax.broadcasted_iota(jnp.int32, (ROPE_DIM, HEAD_PAD), 1)).astype(BF16)
    krp = jnp.dot(kr_ref[...].astype(BF16), place, preferred_element_type=F32)
    c = c_ref[...].astype(BF16)
    _expand_keys(c, krp, wk_ref, kg_ref, k_scr)
    _expand_values(c, wv_ref, v_scr)

    q = q_ref[...]
    m, acc = _online_update(_scores(q, k_scr[...]), v_scr[...], m_scr[...], acc_scr[...])
    m_scr[...] = m
    acc_scr[...] = acc

    @pl.when(kb == last)
    def _():
        _, acc_f = _online_update(_scores(q, kn_ref[...]), vn_ref[...], m, acc)
        for pair in range(N_HEADS // 2):
            o_ref[:, pair * LANES:(pair + 1) * LANES] = _pair_output(acc_f[2 * pair], acc_f[2 * pair + 1])


def _sample_attn_call(q, cache_c, cache_kr, layer, wk, wv, kg, kn, vn, batch, n_new, past, tk=512):
    tk = min(tk, past)
    nk = past // tk
    new_spec = pl.BlockSpec((N_HEADS, n_new, HEAD_PAD), lambda b, j: (0, b, 0))
    cache = lambda w: pl.BlockSpec((None, tk, w), lambda b, j: (layer, b * nk + j, 0))
    return pl.pallas_call(
        _sample_attn_kernel, grid=(batch, nk),
        in_specs=[new_spec, cache(KV_LORA), cache(ROPE_DIM), _const_spec((KV_LORA, Q_PAD)),
                  _const_spec((KV_LORA, Q_PAD)), _const_spec((1, HEAD_PAD)), new_spec, new_spec],
        out_specs=pl.BlockSpec((n_new, D_MODEL), lambda b, j: (b, 0)),
        out_shape=jax.ShapeDtypeStruct((batch * n_new, D_MODEL), BF16),
        scratch_shapes=[pltpu.VMEM((N_HEADS, tk, HEAD_PAD), BF16),
                        pltpu.VMEM((N_HEADS, tk, HEAD_PAD), BF16),
                        pltpu.VMEM((N_HEADS, n_new, 1), F32),
                        pltpu.VMEM((N_HEADS, n_new, HEAD_PAD), F32)],
        compiler_params=_params(2), name="sample_attention")(q, cache_c, cache_kr, wk, wv, kg, kn, vn)


def _lane_vec(parts):
    v = jnp.concatenate(parts)
    return jnp.pad(v, (0, LANES - v.shape[0])).reshape(1, LANES)


def _rope_tables(pos):
    inv = ROPE_BASE ** (-jnp.arange(HALF_ROPE, dtype=F32) / HALF_ROPE)
    ang = pos.astype(F32)[:, None] * inv[None, :]
    cos, sin = jnp.cos(ang), jnp.sin(ang)
    n = pos.shape[0]
    zeros = lambda w: jnp.zeros((n, w), F32)
    cos_t = jnp.concatenate([jnp.ones((n, NOPE_DIM), F32), cos, cos, zeros(ROPE_TAIL)], axis=1)
    sin_t = jnp.concatenate([zeros(ROPE_LANE0), -sin, sin, zeros(ROPE_TAIL)], axis=1)
    return cos_t, sin_t, cos.T, sin.T


def _prep_mla(w_in, w_uq, w_ukv, q_nope_g, q_rope_g, k_nope_g, k_rope_g):
    zeros = lambda n: jnp.zeros((n,), F32)
    swap = lambda g: jnp.concatenate([g[HALF_ROPE:], g[:HALF_ROPE]])
    w_q, w_kv = w_in[:, :Q_LORA], w_in[:, Q_LORA:Q_LORA + KV_LORA]
    w_kr1 = w_in[:, Q_LORA + KV_LORA:Q_LORA + KV_LORA + HALF_ROPE]
    w_kr2 = w_in[:, Q_LORA + KV_LORA + HALF_ROPE:]
    zc = lambda n: jnp.zeros((D_MODEL, n), F32)
    win = jnp.concatenate([w_q, w_kv, zc(ROPE_LANE0), w_kr1, w_kr2, zc(ROPE_TAIL),
                           zc(ROPE_LANE0), w_kr2, w_kr1, zc(ROPE_TAIL)], axis=1).astype(BF16)
    gains = dict(krg=_lane_vec([zeros(ROPE_LANE0), k_rope_g]),
                 krgs=_lane_vec([zeros(ROPE_LANE0), swap(k_rope_g)]),
                 qhg=_lane_vec([q_nope_g, q_rope_g]) * (ATTN_SCALE * LOG2E),
                 qhgs=_lane_vec([zeros(ROPE_LANE0), swap(q_rope_g)]) * (ATTN_SCALE * LOG2E))

    uq = w_uq.reshape(Q_LORA, N_HEADS, QK_DIM)
    uq_n, uq_1, uq_2 = uq[..., :NOPE_DIM], uq[..., NOPE_DIM:NOPE_DIM + HALF_ROPE], uq[..., NOPE_DIM + HALF_ROPE:]
    zq = lambda n: jnp.zeros((Q_LORA, N_HEADS, n), F32)
    main = jnp.concatenate([uq_n, uq_1, uq_2, zq(ROPE_TAIL)], axis=-1).reshape(Q_LORA, Q_PAD)
    swapped = jnp.concatenate([zq(ROPE_LANE0), uq_2, uq_1, zq(ROPE_TAIL)], axis=-1).reshape(Q_LORA, Q_PAD)
    wuq = jnp.concatenate([main, swapped], axis=1).astype(BF16)
    wuq_t = main.T.astype(BF16)

    ukv = w_ukv.reshape(KV_LORA, N_HEADS // 2, 2, NOPE_DIM + V_DIM)
    zk = jnp.zeros((KV_LORA, N_HEADS // 2, 2, HEAD_PAD - NOPE_DIM), F32)
    wk = jnp.concatenate([ukv[..., :NOPE_DIM], zk], axis=-1).reshape(KV_LORA, Q_PAD).astype(BF16)
    zv = jnp.zeros((KV_LORA, N_HEADS // 2, V_DIM), F32)
    v_even = jnp.concatenate([ukv[:, :, 0, NOPE_DIM:], zv], axis=-1)
    v_odd = jnp.concatenate([zv, ukv[:, :, 1, NOPE_DIM:]], axis=-1)
    wv = jnp.stack([v_even, v_odd], axis=2).reshape(KV_LORA, Q_PAD).astype(BF16)
    gains.update(q_nope=q_nope_g * (ATTN_SCALE * LOG2E), q_rope=q_rope_g * (ATTN_SCALE * LOG2E))
    return win, wuq, wuq_t, wk, wv, _lane_vec([k_nope_g]), gains


def _key_tables(cos_t, sin_t, gains):
    return cos_t * gains["krg"], sin_t * gains["krgs"]


def _query_tables(cos_t, sin_t, gains):
    return cos_t * gains["qhg"], sin_t * gains["qhgs"]


def _query_tables_t(cos_tt, sin_tt, gains, tm):
    g1, g2 = gains["q_rope"][:HALF_ROPE, None], gains["q_rope"][HALF_ROPE:, None]
    rot = jnp.concatenate([g1 * cos_tt, g2 * sin_tt, g2 * cos_tt, g1 * sin_tt], axis=0)
    return jnp.broadcast_to(gains["q_nope"][:, None], (NOPE_DIM, tm)), rot


def kernel(x_prompt, x_sample, cache_kv_latent, cache_k_rope, mix_norm_g, mlp_norm_g, sgu_w_in, sgu_norm_g, sgu_w_s, sgu_b_s, sgu_w_out, mla_w_in, mla_q_norm_g, mla_kv_norm_g, mla_w_uq, mla_w_ukv, mla_q_nope_g, mla_q_rope_g, mla_k_nope_g, mla_k_rope_g, mla_w_o, mlp_w_up, mlp_w_down):
    batch, seq, _ = x_prompt.shape
    dec_batch, n_new, _ = x_sample.shape
    past = cache_kv_latent.shape[2]
    depth = mix_norm_g.shape[0]
    rows_p, rows_s = batch * seq, dec_batch * n_new

    xp = x_prompt.reshape(rows_p, D_MODEL)
    xs = x_sample.reshape(rows_s, D_MODEL)
    cos_p, sin_p, cos_pt, sin_pt = _rope_tables(jnp.arange(seq))
    cos_s, sin_s, _, _ = _rope_tables(past + jnp.arange(n_new))
    cos_s, sin_s = jnp.tile(cos_s, (dec_batch, 1)), jnp.tile(sin_s, (dec_batch, 1))
    mlp_wu, mlp_wd = mlp_w_up.astype(BF16), mlp_w_down.astype(BF16)
    sgu_win, sgu_wout = sgu_w_in.astype(BF16), sgu_w_out.astype(BF16)
    mla_wo = mla_w_o.astype(BF16)
    reps = SGU_CHUNK // n_new
    sgu_ws_s = jnp.tile(sgu_w_s[:, :, :n_new, :n_new], (1, 1, reps, reps))
    sgu_b_p = sgu_b_s[..., None]
    sgu_b_smp = jnp.tile(sgu_b_s[:, :, :n_new], (1, 1, reps))[..., None]

    n_mla = mla_w_in.shape[0]
    cache_c = cache_kv_latent.reshape(n_mla, dec_batch * past, KV_LORA)
    cache_kr = cache_k_rope.reshape(n_mla, dec_batch * past, ROPE_DIM)
    cache_p = cache_s = None
    sgu_v_s = []
    for i in range(depth):
        j = i // 2
        g_mix = mix_norm_g[i].reshape(1, D_MODEL)
        g_mlp = mlp_norm_g[i].reshape(1, D_MODEL)
        wu, wd = mlp_wu[i], mlp_wd[i]
        if i % 2 == 0:
            ng = sgu_norm_g[j].reshape(1, D_SGU)
            xp = _sgu_call(xp, g_mix, sgu_win[j], ng, sgu_w_s[j], sgu_b_p[j], sgu_wout[j],
                           block_diag=False, emit_v=False)
            xs, v_new = _sgu_call(xs, g_mix, sgu_win[j], ng, sgu_ws_s[j], sgu_b_smp[j], sgu_wout[j],
                                  block_diag=True, emit_v=True)
            sgu_v_s.append(v_new.reshape(dec_batch, n_new, D_SGU))
            xp = _mlp_call(xp, g_mlp, wu, wd)
            xs = _mlp_call(xs, g_mlp, wu, wd)
        else:
            win, wuq, wuq_t, wk, wv, kg, gains = _prep_mla(
                mla_w_in[j], mla_w_uq[j], mla_w_ukv[j], mla_q_nope_g[j], mla_q_rope_g[j],
                mla_k_nope_g[j], mla_k_rope_g[j])
            qg, kvg = mla_q_norm_g[j].reshape(1, -1), mla_kv_norm_g[j].reshape(1, -1)
            *cache_p, krpad, q_t = _mla_proj_call(
                xp, g_mix, win, qg, kvg, _key_tables(cos_p, sin_p, gains), wuq_t,
                _query_tables_t(cos_pt, sin_pt, gains, min(ATTN_BLOCK, seq)), j, n_mla, cache_p, q_transposed=True)
            k, vt = _kv_expand_call(cache_p[0], j, krpad, wk, wv.T, kg, transposed_values=True)
            attn = _prompt_attn_call(q_t, k, vt, batch, seq)
            xp = _mlp_call(xp, g_mlp, wu, wd, attn=attn, wo=mla_wo[j])
            *cache_s, krpad, q = _mla_proj_call(
                xs, g_mix, win, qg, kvg, _key_tables(cos_s, sin_s, gains), wuq,
                _query_tables(cos_s, sin_s, gains), j, n_mla, cache_s, q_transposed=False)
            kn, vn = _kv_expand_call(cache_s[0], j, krpad, wk, wv, kg, transposed_values=False)
            attn = _sample_attn_call(q, cache_c, cache_kr, j, wk, wv, kg, kn, vn, dec_batch, n_new, past)
            xs = _mlp_call(xs, g_mlp, wu, wd, attn=attn, wo=mla_wo[j])
    return (xp.reshape(batch, seq, D_MODEL), xs.reshape(dec_batch, n_new, D_MODEL),
            cache_p[0].reshape(n_mla, batch, seq, KV_LORA), cache_p[1].reshape(n_mla, batch, seq, ROPE_DIM),
            cache_s[0].reshape(n_mla, dec_batch, n_new, KV_LORA),
            cache_s[1].reshape(n_mla, dec_batch, n_new, ROPE_DIM), jnp.stack(sgu_v_s))
```

```python
import functools
import math

import jax
import jax.numpy as jnp
import numpy as np
from jax import lax
from jax.experimental import pallas as pl
from jax.experimental.pallas import tpu as pltpu

F32 = jnp.float32
BF16 = jnp.bfloat16

D_MODEL = 1024
CHUNK = 64
SGU_CHUNK = 128
D_SGU = 2 * D_MODEL
SGU_GROUP_DIM = 128
SGU_GROUPS = D_SGU // SGU_GROUP_DIM
N_HEADS = 16
Q_LORA = 384
KV_LORA = 256
NOPE_DIM = 64
ROPE_DIM = 32
HALF_ROPE = ROPE_DIM // 2
V_DIM = 64
QK_DIM = NOPE_DIM + ROPE_DIM
ATTN_SCALE = 1.0 / math.sqrt(QK_DIM)
ROPE_BASE = 10000.0
D_FF = 4 * D_MODEL
EPS = 1e-6
NEG_INF = -1e30

LANES = 128
HEAD_PAD = LANES
FF_CHUNK = 1024
SGU_COL_CHUNK = 512
VMEM_LIMIT = 56 * 1024 * 1024
ATTN_BLOCK = 512


def _params(n_axes):
    return pltpu.CompilerParams(
        dimension_semantics=("arbitrary",) * n_axes, vmem_limit_bytes=VMEM_LIMIT)


def _const_spec(shape):
    nd = len(shape)
    return pl.BlockSpec(shape, lambda *_: (0,) * nd, pipeline_mode=pl.Buffered(1))


def _layer_spec(stack, layer):
    shape = stack.shape[1:]
    return pl.BlockSpec((None,) + shape, lambda *_: (layer,) + (0,) * len(shape),
                        pipeline_mode=pl.Buffered(1))


def _row_tile(rows, want):
    tm = min(rows, want)
    assert rows % tm == 0
    return tm


def _rms(xf, g, n):
    ss = jnp.sum(xf * xf, axis=-1, keepdims=True)
    return xf * lax.rsqrt(ss * (1.0 / n) + EPS) * g


def _gelu(z):
    return 0.5 * z * (1.0 + lax.erf(z * np.float32(math.sqrt(0.5))))


def _mlp_delta(x, g_ref, wu_ref, wd_ref):
    h = _rms(x, g_ref[...], D_MODEL).astype(BF16)
    acc = None
    for f in range(D_FF // FF_CHUNK):
        sl = slice(f * FF_CHUNK, (f + 1) * FF_CHUNK)
        a = jnp.dot(h, wu_ref[:, sl], preferred_element_type=F32)
        a = jnp.square(jnp.maximum(a, 0.0)).astype(BF16)
        d = jnp.dot(a, wd_ref[sl, :], preferred_element_type=F32)
        acc = d if acc is None else acc + d
    return acc


def _mlp_kernel(x_ref, g_ref, wu_ref, wd_ref, o_ref):
    x = x_ref[...]
    o_ref[...] = x + _mlp_delta(x, g_ref, wu_ref, wd_ref)


def _attn_out_mlp_kernel(x_ref, a_ref, wo_ref, g_ref, wu_ref, wd_ref, o_ref):
    x = x_ref[...] + jnp.dot(a_ref[...], wo_ref[...], preferred_element_type=F32)
    o_ref[...] = x + _mlp_delta(x, g_ref, wu_ref, wd_ref)


def _mlp_call(x, g, wu, wd, layer, attn=None, wo=None, wo_layer=None, tm_want=512):
    rows = x.shape[0]
    tm = _row_tile(rows, tm_want)
    row_spec = pl.BlockSpec((tm, D_MODEL), lambda i: (i, 0))
    w_specs = [_const_spec((1, D_MODEL)), _layer_spec(wu, layer), _layer_spec(wd, layer)]
    if attn is None:
        kern, ins, specs = _mlp_kernel, (x, g, wu, wd), [row_spec] + w_specs
        name = "channel_mlp"
    else:
        kern, ins = _attn_out_mlp_kernel, (x, attn, wo, g, wu, wd)
        specs = [row_spec, row_spec, _layer_spec(wo, wo_layer)] + w_specs
        name = "attn_out_channel_mlp"
    return pl.pallas_call(
        kern, grid=(rows // tm,), in_specs=specs, out_specs=row_spec,
        out_shape=jax.ShapeDtypeStruct((rows, D_MODEL), F32),
        compiler_params=_params(1), name=name)(*ins)


def _sgu_kernel(x_ref, g_ref, win_ref, ng_ref, ws_ref, b_ref, wout_ref, *rest, block_diag, emit_v):
    if emit_v:
        o_ref, v_ref, u_scr, v_scr, y_scr = rest
    else:
        o_ref, u_scr, v_scr, y_scr = rest
        v_ref = None
    tm = x_ref.shape[0]
    n_chunks = tm // SGU_CHUNK
    x = x_ref[...]
    h = _rms(x, g_ref[...], D_MODEL).astype(BF16)

    ss = jnp.zeros((tm, 1), F32)
    for k in range(D_SGU // SGU_COL_CHUNK):
        sl = slice(k * SGU_COL_CHUNK, (k + 1) * SGU_COL_CHUNK)
        u_scr[:, sl] = _gelu(jnp.dot(h, win_ref[:, sl], preferred_element_type=F32))
        slv = slice(D_SGU + k * SGU_COL_CHUNK, D_SGU + (k + 1) * SGU_COL_CHUNK)
        zv = _gelu(jnp.dot(h, win_ref[:, slv], preferred_element_type=F32))
        ss = ss + jnp.sum(zv * zv, axis=-1, keepdims=True)
        v_scr[:, sl] = zv
    r = lax.rsqrt(ss * (1.0 / D_SGU) + EPS)

    ri = lax.broadcasted_iota(jnp.int32, (SGU_CHUNK, SGU_CHUNK), 0) // CHUNK
    ci = lax.broadcasted_iota(jnp.int32, (SGU_CHUNK, SGU_CHUNK), 1) // CHUNK
    mask = (ri == ci) if block_diag else (ri >= ci)
    for grp in range(SGU_GROUPS):
        gsl = slice(grp * SGU_GROUP_DIM, (grp + 1) * SGU_GROUP_DIM)
        ws = jnp.where(mask, ws_ref[grp], 0.0).astype(BF16)
        pieces = []
        for c in range(n_chunks):
            rsl = slice(c * SGU_CHUNK, (c + 1) * SGU_CHUNK)
            vn = v_scr[rsl, gsl] * r[rsl] * ng_ref[:, gsl]
            if emit_v:
                v_ref[rsl, gsl] = vn
            pieces.append(vn.astype(BF16))
        rhs = pieces[0] if n_chunks == 1 else jnp.concatenate(pieces, axis=1)
        mixed = jnp.dot(ws, rhs, preferred_element_type=F32) + b_ref[grp]
        for c in range(n_chunks):
            rsl = slice(c * SGU_CHUNK, (c + 1) * SGU_CHUNK)
            y = u_scr[rsl, gsl] * mixed[:, c * SGU_CHUNK:(c + 1) * SGU_CHUNK]
            y_scr[rsl, gsl] = y.astype(BF16)
    o_ref[...] = x + jnp.dot(y_scr[...], wout_ref[...], preferred_element_type=F32)


def _sgu_call(x, g, win, ng, ws, b, wout, layer, *, block_diag, emit_v, tm_want=512):
    rows = x.shape[0]
    tm = _row_tile(rows, tm_want)
    row_spec = pl.BlockSpec((tm, D_MODEL), lambda i: (i, 0))
    specs = [row_spec, _const_spec((1, D_MODEL)), _layer_spec(win, layer), _const_spec((1, D_SGU)),
             _layer_spec(ws, layer), _layer_spec(b, layer), _layer_spec(wout, layer)]
    out_shape = [jax.ShapeDtypeStruct((rows, D_MODEL), F32)]
    out_specs = [row_spec]
    if emit_v:
        out_shape.append(jax.ShapeDtypeStruct((rows, D_SGU), F32))
        out_specs.append(pl.BlockSpec((tm, D_SGU), lambda i: (i, 0)))
    res = pl.pallas_call(
        functools.partial(_sgu_kernel, block_diag=block_diag, emit_v=emit_v),
        grid=(rows // tm,), in_specs=specs, out_specs=out_specs, out_shape=out_shape,
        scratch_shapes=[pltpu.VMEM((tm, D_SGU), F32), pltpu.VMEM((tm, D_SGU), F32),
                        pltpu.VMEM((tm, D_SGU), BF16)],
        compiler_params=_params(1), name="sgu_mixer_v" if emit_v else "sgu_mixer")(
            x, g, win, ng, ws, b, wout)
    return res if emit_v else res[0]


ROPE_LANE0 = NOPE_DIM
ROPE_TAIL = HEAD_PAD - ROPE_LANE0 - ROPE_DIM
W_IN_PAD = Q_LORA + KV_LORA + 2 * LANES
Q_PAD = N_HEADS * HEAD_PAD
LOG2E = math.log2(math.e)


def _ones_lane(hd):
    return V_DIM if hd % 2 == 0 else 0


def _mla_proj_kernel(x_ref, g_ref, win_ref, qg_ref, kvg_ref, kcos_ref, ksin_ref, wuq_ref,
                     qtab_a_ref, qtab_b_ref, *rest, prompt_layout):
    if prompt_layout:
        wk_ref, wvt_ref, kg_ref = rest[:3]
        ckv_ref, kr_ref, q_ref, k_ref, vt_ref = rest[-5:]
    else:
        ckv_ref, kr_ref, krpad_ref, q_ref = rest[-4:]
    x = x_ref[...]
    h = _rms(x, g_ref[...], D_MODEL).astype(BF16)
    a = jnp.dot(h, win_ref[...], preferred_element_type=F32)
    c_q = _rms(a[:, :Q_LORA], qg_ref[...], Q_LORA).astype(BF16)
    c_kv = _rms(a[:, Q_LORA:Q_LORA + KV_LORA], kvg_ref[...], KV_LORA)
    ckv_ref[...] = c_kv

    kr = a[:, Q_LORA + KV_LORA:Q_LORA + KV_LORA + LANES]
    kr_sw = a[:, Q_LORA + KV_LORA + LANES:]
    r_kr = lax.rsqrt(jnp.sum(kr * kr, axis=-1, keepdims=True) * (1.0 / ROPE_DIM) + EPS)
    kr_rot = (kr * r_kr) * kcos_ref[...] + (kr_sw * r_kr) * ksin_ref[...]
    kr_ref[...] = kr_rot[:, ROPE_LANE0:ROPE_LANE0 + ROPE_DIM]

    if prompt_layout:
        c = c_kv.astype(BF16)
        _expand_keys(c, kr_rot, wk_ref, kg_ref, k_ref)
        _expand_values_t(c, wvt_ref, vt_ref)
        q_t = lax.dot_general(wuq_ref[...], c_q, (((1,), (1,)), ((), ())), preferred_element_type=F32)
        g_nope = qtab_a_ref[...]
        rot = qtab_b_ref[...]
        t_a, t_b, t_c, t_d = (rot[i * HALF_ROPE:(i + 1) * HALF_ROPE] for i in range(4))
        pad = jnp.zeros((ROPE_TAIL, q_t.shape[1]), F32)
        for hd in range(N_HEADS):
            base = hd * HEAD_PAD
            nope = q_t[base:base + NOPE_DIM]
            x1 = q_t[base + NOPE_DIM:base + NOPE_DIM + HALF_ROPE]
            x2 = q_t[base + NOPE_DIM + HALF_ROPE:base + QK_DIM]
            s_n = jnp.sum(nope * nope, axis=0, keepdims=True)
            s_r = jnp.sum(x1 * x1 + x2 * x2, axis=0, keepdims=True)
            r_n = lax.rsqrt(s_n * (1.0 / NOPE_DIM) + EPS)
            r_r = lax.rsqrt(s_r * (1.0 / ROPE_DIM) + EPS)
            x1, x2 = x1 * r_r, x2 * r_r
            q_ref[hd] = jnp.concatenate(
                [nope * r_n * g_nope, x1 * t_a - x2 * t_b, x2 * t_c + x1 * t_d, pad], axis=0).astype(BF16)
        return

    krpad_ref[...] = kr_rot.astype(BF16)
    qq = jnp.dot(c_q, wuq_ref[...], preferred_element_type=F32)
    qcos = qtab_a_ref[...]
    qsin = qtab_b_ref[...]
    is_nope = lax.broadcasted_iota(jnp.int32, (1, HEAD_PAD), 1) < NOPE_DIM
    for hd in range(N_HEADS):
        blk = qq[:, hd * HEAD_PAD:(hd + 1) * HEAD_PAD]
        blk_sw = qq[:, Q_PAD + hd * HEAD_PAD:Q_PAD + (hd + 1) * HEAD_PAD]
        sq = blk * blk
        s_n = jnp.sum(jnp.where(is_nope, sq, 0.0), axis=-1, keepdims=True)
        s_r = jnp.sum(jnp.where(is_nope, 0.0, sq), axis=-1, keepdims=True)
        r_n = lax.rsqrt(s_n * (1.0 / NOPE_DIM) + EPS)
        r_r = lax.rsqrt(s_r * (1.0 / ROPE_DIM) + EPS)
        q_ref[hd] = ((blk * jnp.where(is_nope, r_n, r_r)) * qcos + (blk_sw * r_r) * qsin).astype(BF16)


def _mla_proj_call(x, g, win, qg, kvg, k_tables, wuq, q_tables, layer, n_layers, prev=None, *,
                   kv_weights=None, tm_want=ATTN_BLOCK):
    prompt_layout = kv_weights is not None
    rows = x.shape[0]
    period = k_tables[0].shape[0]
    tm = _row_tile(min(rows, period), tm_want)
    n_per = period // tm
    row = lambda w: pl.BlockSpec((tm, w), lambda i: (i, 0))
    slot = lambda w: pl.BlockSpec((None, tm, w), lambda i: (layer, i, 0))
    tab = pl.BlockSpec((tm, HEAD_PAD), lambda i: (i % n_per, 0))
    head = pl.BlockSpec((N_HEADS, tm, HEAD_PAD), lambda i: (0, i, 0))
    head_shape = jax.ShapeDtypeStruct((N_HEADS, rows, HEAD_PAD), BF16)
    specs = [row(D_MODEL), _const_spec((1, D_MODEL)), _const_spec((D_MODEL, W_IN_PAD)),
             _const_spec((1, Q_LORA)), _const_spec((1, KV_LORA)), tab, tab, _const_spec(wuq.shape)]
    ins = [x, g, win, qg, kvg, *k_tables, wuq, *q_tables]
    out_shape = [jax.ShapeDtypeStruct((n_layers, rows, KV_LORA), F32),
                 jax.ShapeDtypeStruct((n_layers, rows, ROPE_DIM), F32)]
    out_specs = [slot(KV_LORA), slot(ROPE_DIM)]
    if prompt_layout:
        specs += [_const_spec((NOPE_DIM, tm)), pl.BlockSpec((2 * ROPE_DIM, tm), lambda i: (0, i % n_per))]
        specs += [_const_spec(w.shape) for w in kv_weights]
        ins += list(kv_weights)
        out_shape += [jax.ShapeDtypeStruct((N_HEADS, HEAD_PAD, rows), BF16), head_shape,
                      jax.ShapeDtypeStruct((N_HEADS, rows // tm, HEAD_PAD, tm), BF16)]
        out_specs += [pl.BlockSpec((N_HEADS, HEAD_PAD, tm), lambda i: (0, 0, i)), head,
                      pl.BlockSpec((N_HEADS, 1, HEAD_PAD, tm), lambda i: (0, i, 0, 0))]
    else:
        specs += [tab, tab]
        out_shape += [jax.ShapeDtypeStruct((rows, LANES), BF16), head_shape]
        out_specs += [row(LANES), head]
    aliases = {}
    if prev is not None:
        aliases = {len(ins): 0, len(ins) + 1: 1}
        ins += list(prev)
        specs += [pl.BlockSpec(memory_space=pl.ANY)] * 2
    return pl.pallas_call(
        functools.partial(_mla_proj_kernel, prompt_layout=prompt_layout), grid=(rows // tm,),
        in_specs=specs, out_specs=out_specs, out_shape=out_shape, input_output_aliases=aliases,
        compiler_params=_params(1), name="mla_project_kv" if prompt_layout else "mla_project")(*ins)


def _expand_keys(c, krp, wk_ref, kg_ref, k_dst):
    kk = jnp.dot(c, wk_ref[...], preferred_element_type=F32)
    for hd in range(N_HEADS):
        blk = kk[:, hd * HEAD_PAD:(hd + 1) * HEAD_PAD]
        r = lax.rsqrt(jnp.sum(blk * blk, axis=-1, keepdims=True) * (1.0 / NOPE_DIM) + EPS)
        k_dst[hd] = (blk * r * kg_ref[...] + krp).astype(BF16)


def _expand_values(c, wv_ref, v_dst):
    vv = jnp.dot(c, wv_ref[...], preferred_element_type=F32)
    lane = lax.broadcasted_iota(jnp.int32, (1, HEAD_PAD), 1)
    for hd in range(N_HEADS):
        vblk = vv[:, hd * HEAD_PAD:(hd + 1) * HEAD_PAD]
        v_dst[hd] = jnp.where(lane == _ones_lane(hd), 1.0, vblk).astype(BF16)


def _expand_values_t(c, wvt_ref, vt_dst):
    vt = lax.dot_general(wvt_ref[...], c, (((1,), (1,)), ((), ())), preferred_element_type=F32)
    row = lax.broadcasted_iota(jnp.int32, (HEAD_PAD, 1), 0)
    for hd in range(N_HEADS):
        blk = vt[hd * HEAD_PAD:(hd + 1) * HEAD_PAD, :]
        vt_dst[hd, 0] = jnp.where(row == _ones_lane(hd), 1.0, blk).astype(BF16)


def _kv_expand_kernel(c_ref, krpad_ref, wk_ref, wv_ref, kg_ref, k_ref, v_ref):
    c = c_ref[...].astype(BF16)
    _expand_keys(c, krpad_ref[...].astype(F32), wk_ref, kg_ref, k_ref)
    _expand_values(c, wv_ref, v_ref)


def _kv_expand_call(c_stack, layer, krpad, wk, wv, kg, tm_want=512):
    rows = c_stack.shape[1]
    tm = _row_tile(rows, tm_want)
    row = lambda w: pl.BlockSpec((tm, w), lambda i: (i, 0))
    c_spec = pl.BlockSpec((None, tm, KV_LORA), lambda i: (layer, i, 0))
    head = pl.BlockSpec((N_HEADS, tm, HEAD_PAD), lambda i: (0, i, 0))
    hs = jax.ShapeDtypeStruct((N_HEADS, rows, HEAD_PAD), BF16)
    return pl.pallas_call(
        _kv_expand_kernel, grid=(rows // tm,),
        in_specs=[c_spec, row(LANES), _const_spec(wk.shape), _const_spec(wv.shape),
                  _const_spec((1, HEAD_PAD))],
        out_specs=[head, head], out_shape=[hs, hs],
        compiler_params=_params(1), name="kv_expand")(c_stack, krpad, wk, wv, kg)


def _scores(q, k):
    batch_dims = tuple(range(q.ndim - 2))
    return lax.dot_general(q, k, (((q.ndim - 1,), (k.ndim - 1,)), (batch_dims, batch_dims)),
                           preferred_element_type=F32)


def _online_update(s, v, m, acc):
    m_new = jnp.maximum(m, jnp.max(s, axis=-1, keepdims=True))
    alpha = jnp.exp2(m - m_new)
    p = jnp.exp2(s - m_new).astype(BF16)
    batch_dims = tuple(range(p.ndim - 2))
    pv = lax.dot_general(p, v, (((p.ndim - 1,), (v.ndim - 2,)), (batch_dims, batch_dims)),
                         preferred_element_type=F32)
    return m_new, alpha * acc + pv


def _pair_output(acc_even, acc_odd):
    l_even = acc_even[:, _ones_lane(0):_ones_lane(0) + 1]
    l_odd = acc_odd[:, _ones_lane(1):_ones_lane(1) + 1]
    lane = lax.broadcasted_iota(jnp.int32, (1, HEAD_PAD), 1)
    return jnp.where(lane < V_DIM, acc_even / l_even, acc_odd / l_odd).astype(BF16)


HEADS_PER_STEP = 4
DIAG_STRIP = 256


def _prompt_attn_kernel(qt_ref, k_ref, vt_ref, o_ref, *, tq):
    seq = k_ref.shape[1]
    row = lax.broadcasted_iota(jnp.int32, (HEAD_PAD, 1), 0)
    heads = range(HEADS_PER_STEP)

    def update(s, vt, m, acc):
        m_new = jnp.maximum(m, jnp.max(s, axis=0, keepdims=True))
        alpha = jnp.exp2(m - m_new)
        p = jnp.exp2(s - m_new).astype(BF16)
        return m_new, alpha * acc + jnp.dot(vt, p, preferred_element_type=F32)

    for qi in range(seq // tq):
        q0 = qi * tq
        carry = [(jnp.full((1, tq), NEG_INF, F32), jnp.zeros((HEAD_PAD, tq), F32)) for _ in heads]
        for kb in range(qi):
            ss = [jnp.dot(k_ref[hd, kb * tq:(kb + 1) * tq, :], qt_ref[hd, :, q0:q0 + tq],
                          preferred_element_type=F32) for hd in heads]
            carry = [update(s, vt_ref[hd, kb], m, acc) for s, hd, (m, acc) in zip(ss, heads, carry)]
        for j in range(tq // DIAG_STRIP):
            c0 = j * DIAG_STRIP
            n_keys = c0 + DIAG_STRIP
            key_chunk = lax.broadcasted_iota(jnp.int32, (n_keys, DIAG_STRIP), 0) // CHUNK
            qry_chunk = (lax.broadcasted_iota(jnp.int32, (n_keys, DIAG_STRIP), 1) + c0) // CHUNK
            mask = key_chunk <= qry_chunk
            ss = [jnp.dot(k_ref[hd, q0:q0 + n_keys, :], qt_ref[hd, :, q0 + c0:q0 + n_keys],
                          preferred_element_type=F32) for hd in heads]
            accs = [update(jnp.where(mask, s, NEG_INF), vt_ref[hd, qi, :, :n_keys],
                           m[:, c0:c0 + DIAG_STRIP], acc[:, c0:c0 + DIAG_STRIP])[1]
                    for s, hd, (m, acc) in zip(ss, heads, carry)]
            for p in range(HEADS_PER_STEP // 2):
                acc_e, acc_o = accs[2 * p], accs[2 * p + 1]
                l_e = acc_e[_ones_lane(0):_ones_lane(0) + 1, :]
                l_o = acc_o[_ones_lane(1):_ones_lane(1) + 1, :]
                o_t = jnp.where(row < V_DIM, acc_e / l_e, acc_o / l_o)
                o_ref[q0 + c0:q0 + n_keys, p * LANES:(p + 1) * LANES] = o_t.T.astype(BF16)


def _prompt_attn_call(q_t, k, vt, batch, seq, tq=ATTN_BLOCK):
    tq = min(tq, seq)
    nq = seq // tq
    rows = batch * seq
    assert vt.shape == (N_HEADS, batch * nq, HEAD_PAD, tq)
    g = HEADS_PER_STEP
    return pl.pallas_call(
        functools.partial(_prompt_attn_kernel, tq=tq),
        grid=(batch, N_HEADS // g),
        in_specs=[pl.BlockSpec((g, HEAD_PAD, seq), lambda b, i: (i, 0, b)),
                  pl.BlockSpec((g, seq, HEAD_PAD), lambda b, i: (i, b, 0)),
                  pl.BlockSpec((g, nq, HEAD_PAD, tq), lambda b, i: (i, b, 0, 0))],
        out_specs=pl.BlockSpec((seq, g // 2 * LANES), lambda b, i: (b, i)),
        out_shape=jax.ShapeDtypeStruct((rows, D_MODEL), BF16),
        compiler_params=_params(2), name="prompt_attention")(q_t, k, vt)


def _sample_attn_kernel(q_ref, c_ref, kr_ref, wk_ref, wv_ref, kg_ref, kn_ref, vn_ref, o_ref,
                        k_scr, v_scr, m_scr, acc_scr):
    kb = pl.program_id(1)
    last = pl.num_programs(1) - 1

    @pl.when(kb == 0)
    def _():
        m_scr[...] = jnp.full(m_scr.shape, NEG_INF, F32)
        acc_scr[...] = jnp.zeros(acc_scr.shape, F32)

    place = (lax.broadcasted_iota(jnp.int32, (ROPE_DIM, HEAD_PAD), 0) + ROPE_LANE0
             == lax.broadcasted_iota(jnp.int32, (ROPE_DIM, HEAD_PAD), 1)).astype(BF16)
    krp = jnp.dot(kr_ref[...].astype(BF16), place, preferred_element_type=F32)
    c = c_ref[...].astype(BF16)
    _expand_keys(c, krp, wk_ref, kg_ref, k_scr)
    _expand_values(c, wv_ref, v_scr)

    q = q_ref[...]
    m, acc = _online_update(_scores(q, k_scr[...]), v_scr[...], m_scr[...], acc_scr[...])
    m_scr[...] = m
    acc_scr[...] = acc

    @pl.when(kb == last)
    def _():
        _, acc_f = _online_update(_scores(q, kn_ref[...]), vn_ref[...], m, acc)
        for pair in range(N_HEADS // 2):
            o_ref[:, pair * LANES:(pair + 1) * LANES] = _pair_output(acc_f[2 * pair], acc_f[2 * pair + 1])


def _sample_attn_call(q, cache_c, cache_kr, layer, wk, wv, kg, kn, vn, batch, n_new, past, tk=512):
    tk = min(tk, past)
    nk = past // tk
    new_spec = pl.BlockSpec((N_HEADS, n_new, HEAD_PAD), lambda b, j: (0, b, 0))
    cache = lambda w: pl.BlockSpec((None, None, tk, w), lambda b, j: (layer, b, j, 0))
    return pl.pallas_call(
        _sample_attn_kernel, grid=(batch, nk),
        in_specs=[new_spec, cache(KV_LORA), cache(ROPE_DIM), _const_spec((KV_LORA, Q_PAD)),
                  _const_spec((KV_LORA, Q_PAD)), _const_spec((1, HEAD_PAD)), new_spec, new_spec],
        out_specs=pl.BlockSpec((n_new, D_MODEL), lambda b, j: (b, 0)),
        out_shape=jax.ShapeDtypeStruct((batch * n_new, D_MODEL), BF16),
        scratch_shapes=[pltpu.VMEM((N_HEADS, tk, HEAD_PAD), BF16),
                        pltpu.VMEM((N_HEADS, tk, HEAD_PAD), BF16),
                        pltpu.VMEM((N_HEADS, n_new, 1), F32),
                        pltpu.VMEM((N_HEADS, n_new, HEAD_PAD), F32)],
        compiler_params=_params(2), name="sample_attention")(q, cache_c, cache_kr, wk, wv, kg, kn, vn)


def _lane_vec(parts):
    v = jnp.concatenate(parts)
    return jnp.pad(v, (0, LANES - v.shape[0])).reshape(1, LANES)


def _rope_tables(pos):
    inv = ROPE_BASE ** (-jnp.arange(HALF_ROPE, dtype=F32) / HALF_ROPE)
    ang = pos.astype(F32)[:, None] * inv[None, :]
    cos, sin = jnp.cos(ang), jnp.sin(ang)
    n = pos.shape[0]
    zeros = lambda w: jnp.zeros((n, w), F32)
    cos_t = jnp.concatenate([jnp.ones((n, NOPE_DIM), F32), cos, cos, zeros(ROPE_TAIL)], axis=1)
    sin_t = jnp.concatenate([zeros(ROPE_LANE0), -sin, sin, zeros(ROPE_TAIL)], axis=1)
    return cos_t, sin_t, cos.T, sin.T


def _prep_mla(w_in, w_uq, w_ukv, q_nope_g, q_rope_g, k_nope_g, k_rope_g):
    zeros = lambda n: jnp.zeros((n,), F32)
    swap = lambda g: jnp.concatenate([g[HALF_ROPE:], g[:HALF_ROPE]])
    w_q, w_kv = w_in[:, :Q_LORA], w_in[:, Q_LORA:Q_LORA + KV_LORA]
    w_kr1 = w_in[:, Q_LORA + KV_LORA:Q_LORA + KV_LORA + HALF_ROPE]
    w_kr2 = w_in[:, Q_LORA + KV_LORA + HALF_ROPE:]
    zc = lambda n: jnp.zeros((D_MODEL, n), F32)
    win = jnp.concatenate([w_q, w_kv, zc(ROPE_LANE0), w_kr1, w_kr2, zc(ROPE_TAIL),
                           zc(ROPE_LANE0), w_kr2, w_kr1, zc(ROPE_TAIL)], axis=1).astype(BF16)
    gains = dict(krg=_lane_vec([zeros(ROPE_LANE0), k_rope_g]),
                 krgs=_lane_vec([zeros(ROPE_LANE0), swap(k_rope_g)]),
                 qhg=_lane_vec([q_nope_g, q_rope_g]) * (ATTN_SCALE * LOG2E),
                 qhgs=_lane_vec([zeros(ROPE_LANE0), swap(q_rope_g)]) * (ATTN_SCALE * LOG2E))

    uq = w_uq.reshape(Q_LORA, N_HEADS, QK_DIM)
    uq_n, uq_1, uq_2 = uq[..., :NOPE_DIM], uq[..., NOPE_DIM:NOPE_DIM + HALF_ROPE], uq[..., NOPE_DIM + HALF_ROPE:]
    zq = lambda n: jnp.zeros((Q_LORA, N_HEADS, n), F32)
    main = jnp.concatenate([uq_n, uq_1, uq_2, zq(ROPE_TAIL)], axis=-1).reshape(Q_LORA, Q_PAD)
    swapped = jnp.concatenate([zq(ROPE_LANE0), uq_2, uq_1, zq(ROPE_TAIL)], axis=-1).reshape(Q_LORA, Q_PAD)
    wuq = jnp.concatenate([main, swapped], axis=1).astype(BF16)
    wuq_t = main.T.astype(BF16)

    ukv = w_ukv.reshape(KV_LORA, N_HEADS // 2, 2, NOPE_DIM + V_DIM)
    zk = jnp.zeros((KV_LORA, N_HEADS // 2, 2, HEAD_PAD - NOPE_DIM), F32)
    wk = jnp.concatenate([ukv[..., :NOPE_DIM], zk], axis=-1).reshape(KV_LORA, Q_PAD).astype(BF16)
    zv = jnp.zeros((KV_LORA, N_HEADS // 2, V_DIM), F32)
    v_even = jnp.concatenate([ukv[:, :, 0, NOPE_DIM:], zv], axis=-1)
    v_odd = jnp.concatenate([zv, ukv[:, :, 1, NOPE_DIM:]], axis=-1)
    wv = jnp.stack([v_even, v_odd], axis=2).reshape(KV_LORA, Q_PAD).astype(BF16)
    gains.update(q_nope=q_nope_g * (ATTN_SCALE * LOG2E), q_rope=q_rope_g * (ATTN_SCALE * LOG2E))
    return win, wuq, wuq_t, wk, wv, _lane_vec([k_nope_g]), gains


def _key_tables(cos_t, sin_t, gains):
    return cos_t * gains["krg"], sin_t * gains["krgs"]


def _query_tables(cos_t, sin_t, gains):
    return cos_t * gains["qhg"], sin_t * gains["qhgs"]


def _query_tables_t(cos_tt, sin_tt, gains, tm):
    g1, g2 = gains["q_rope"][:HALF_ROPE, None], gains["q_rope"][HALF_ROPE:, None]
    rot = jnp.concatenate([g1 * cos_tt, g2 * sin_tt, g2 * cos_tt, g1 * sin_tt], axis=0)
    return jnp.broadcast_to(gains["q_nope"][:, None], (NOPE_DIM, tm)), rot


def kernel(x_prompt, x_sample, cache_kv_latent, cache_k_rope, mix_norm_g, mlp_norm_g, sgu_w_in, sgu_norm_g, sgu_w_s, sgu_b_s, sgu_w_out, mla_w_in, mla_q_norm_g, mla_kv_norm_g, mla_w_uq, mla_w_ukv, mla_q_nope_g, mla_q_rope_g, mla_k_nope_g, mla_k_rope_g, mla_w_o, mlp_w_up, mlp_w_down):
    batch, seq, _ = x_prompt.shape
    dec_batch, n_new, _ = x_sample.shape
    past = cache_kv_latent.shape[2]
    depth = mix_norm_g.shape[0]
    rows_p, rows_s = batch * seq, dec_batch * n_new

    xp = x_prompt.reshape(rows_p, D_MODEL)
    xs = x_sample.reshape(rows_s, D_MODEL)
    cos_p, sin_p, cos_pt, sin_pt = _rope_tables(jnp.arange(seq))
    cos_s, sin_s, _, _ = _rope_tables(past + jnp.arange(n_new))
    cos_s, sin_s = jnp.tile(cos_s, (dec_batch, 1)), jnp.tile(sin_s, (dec_batch, 1))
    mlp_wu, mlp_wd = mlp_w_up.astype(BF16), mlp_w_down.astype(BF16)
    sgu_win, sgu_wout = sgu_w_in.astype(BF16), sgu_w_out.astype(BF16)
    mla_wo = mla_w_o.astype(BF16)
    reps = SGU_CHUNK // n_new
    sgu_ws_s = jnp.tile(sgu_w_s[:, :, :n_new, :n_new], (1, 1, reps, reps))
    sgu_b_p = sgu_b_s[..., None]
    sgu_b_smp = jnp.tile(sgu_b_s[:, :, :n_new], (1, 1, reps))[..., None]

    n_mla = mla_w_in.shape[0]
    cache_p = cache_s = None
    sgu_v_s = []
    for i in range(depth):
        j = i // 2
        g_mix = mix_norm_g[i].reshape(1, D_MODEL)
        g_mlp = mlp_norm_g[i].reshape(1, D_MODEL)
        if i % 2 == 0:
            ng = sgu_norm_g[j].reshape(1, D_SGU)
            xp = _sgu_call(xp, g_mix, sgu_win, ng, sgu_w_s, sgu_b_p, sgu_wout, j,
                           block_diag=False, emit_v=False)
            xs, v_new = _sgu_call(xs, g_mix, sgu_win, ng, sgu_ws_s, sgu_b_smp, sgu_wout, j,
                                  block_diag=True, emit_v=True)
            sgu_v_s.append(v_new.reshape(dec_batch, n_new, D_SGU))
            xp = _mlp_call(xp, g_mlp, mlp_wu, mlp_wd, i)
            xs = _mlp_call(xs, g_mlp, mlp_wu, mlp_wd, i)
        else:
            win, wuq, wuq_t, wk, wv, kg, gains = _prep_mla(
                mla_w_in[j], mla_w_uq[j], mla_w_ukv[j], mla_q_nope_g[j], mla_q_rope_g[j],
                mla_k_nope_g[j], mla_k_rope_g[j])
            qg, kvg = mla_q_norm_g[j].reshape(1, -1), mla_kv_norm_g[j].reshape(1, -1)
            *cache_p, q_t, k, vt = _mla_proj_call(
                xp, g_mix, win, qg, kvg, _key_tables(cos_p, sin_p, gains), wuq_t,
                _query_tables_t(cos_pt, sin_pt, gains, min(ATTN_BLOCK, seq)), j, n_mla, cache_p,
                kv_weights=(wk, wv.T, kg))
            attn = _prompt_attn_call(q_t, k, vt, batch, seq)
            xp = _mlp_call(xp, g_mlp, mlp_wu, mlp_wd, i, attn=attn, wo=mla_wo, wo_layer=j)
            *cache_s, krpad, q = _mla_proj_call(
                xs, g_mix, win, qg, kvg, _key_tables(cos_s, sin_s, gains), wuq,
                _query_tables(cos_s, sin_s, gains), j, n_mla, cache_s)
            kn, vn = _kv_expand_call(cache_s[0], j, krpad, wk, wv, kg)
            attn = _sample_attn_call(q, cache_kv_latent, cache_k_rope, j, wk, wv, kg, kn, vn,
                                     dec_batch, n_new, past)
            xs = _mlp_call(xs, g_mlp, mlp_wu, mlp_wd, i, attn=attn, wo=mla_wo, wo_layer=j)
    return (xp.reshape(batch, seq, D_MODEL), xs.reshape(dec_batch, n_new, D_MODEL),
            cache_p[0].reshape(n_mla, batch, seq, KV_LORA), cache_p[1].reshape(n_mla, batch, seq, ROPE_DIM),
            cache_s[0].reshape(n_mla, dec_batch, n_new, KV_LORA),
            cache_s[1].reshape(n_mla, dec_batch, n_new, ROPE_DIM), jnp.stack(sgu_v_s))
```

```python
import functools
import math

import jax
import jax.numpy as jnp
import numpy as np
from jax import lax
from jax.experimental import pallas as pl
from jax.experimental.pallas import tpu as pltpu

F32 = jnp.float32
BF16 = jnp.bfloat16

D_MODEL = 1024
CHUNK = 64
SGU_CHUNK = 128
D_SGU = 2 * D_MODEL
SGU_GROUP_DIM = 128
SGU_GROUPS = D_SGU // SGU_GROUP_DIM
N_HEADS = 16
Q_LORA = 384
KV_LORA = 256
NOPE_DIM = 64
ROPE_DIM = 32
HALF_ROPE = ROPE_DIM // 2
V_DIM = 64
QK_DIM = NOPE_DIM + ROPE_DIM
ATTN_SCALE = 1.0 / math.sqrt(QK_DIM)
ROPE_BASE = 10000.0
D_FF = 4 * D_MODEL
EPS = 1e-6
NEG_INF = -1e30

LANES = 128
HEAD_PAD = LANES
FF_CHUNK = 1024
SGU_COL_CHUNK = 512
VMEM_LIMIT = 56 * 1024 * 1024
ATTN_BLOCK = 512


def _params(n_axes):
    return pltpu.CompilerParams(
        dimension_semantics=("arbitrary",) * n_axes, vmem_limit_bytes=VMEM_LIMIT)


def _const_spec(shape):
    nd = len(shape)
    return pl.BlockSpec(shape, lambda *_: (0,) * nd, pipeline_mode=pl.Buffered(1))


def _layer_spec(stack, layer):
    shape = stack.shape[1:]
    return pl.BlockSpec((None,) + shape, lambda *_: (layer,) + (0,) * len(shape),
                        pipeline_mode=pl.Buffered(1))


def _row_tile(rows, want):
    tm = min(rows, want)
    assert rows % tm == 0
    return tm


def _rms(xf, g, n):
    ss = jnp.sum(xf * xf, axis=-1, keepdims=True)
    return xf * lax.rsqrt(ss * (1.0 / n) + EPS) * g


def _gelu(z):
    return 0.5 * z * (1.0 + lax.erf(z * np.float32(math.sqrt(0.5))))


def _mlp_delta(x, g_ref, wu_ref, wd_ref):
    h = _rms(x, g_ref[...], D_MODEL).astype(BF16)
    acc = None
    for f in range(D_FF // FF_CHUNK):
        sl = slice(f * FF_CHUNK, (f + 1) * FF_CHUNK)
        a = jnp.dot(h, wu_ref[:, sl], preferred_element_type=F32)
        a = jnp.square(jnp.maximum(a, 0.0)).astype(BF16)
        d = jnp.dot(a, wd_ref[sl, :], preferred_element_type=F32)
        acc = d if acc is None else acc + d
    return acc


def _mlp_kernel(x_ref, g_ref, wu_ref, wd_ref, o_ref):
    x = x_ref[...]
    o_ref[...] = x + _mlp_delta(x, g_ref, wu_ref, wd_ref)


def _attn_out_mlp_kernel(x_ref, a_ref, wo_ref, g_ref, wu_ref, wd_ref, o_ref):
    x = x_ref[...] + jnp.dot(a_ref[...], wo_ref[...], preferred_element_type=F32)
    o_ref[...] = x + _mlp_delta(x, g_ref, wu_ref, wd_ref)


def _mlp_call(x, g, wu, wd, layer, attn=None, wo=None, wo_layer=None, tm_want=512):
    rows = x.shape[0]
    tm = _row_tile(rows, tm_want)
    row_spec = pl.BlockSpec((tm, D_MODEL), lambda i: (i, 0))
    w_specs = [_const_spec((1, D_MODEL)), _layer_spec(wu, layer), _layer_spec(wd, layer)]
    if attn is None:
        kern, ins, specs = _mlp_kernel, (x, g, wu, wd), [row_spec] + w_specs
        name = "channel_mlp"
    else:
        kern, ins = _attn_out_mlp_kernel, (x, attn, wo, g, wu, wd)
        specs = [row_spec, row_spec, _layer_spec(wo, wo_layer)] + w_specs
        name = "attn_out_channel_mlp"
    return pl.pallas_call(
        kern, grid=(rows // tm,), in_specs=specs, out_specs=row_spec,
        out_shape=jax.ShapeDtypeStruct((rows, D_MODEL), F32),
        compiler_params=_params(1), name=name)(*ins)


def _sgu_kernel(x_ref, g_ref, win_ref, ng_ref, ws_ref, b_ref, wout_ref, *rest, block_diag, emit_v):
    if emit_v:
        o_ref, v_ref, u_scr, v_scr, y_scr = rest
    else:
        o_ref, u_scr, v_scr, y_scr = rest
        v_ref = None
    tm = x_ref.shape[0]
    n_chunks = tm // SGU_CHUNK
    x = x_ref[...]
    h = _rms(x, g_ref[...], D_MODEL).astype(BF16)

    ss = jnp.zeros((tm, 1), F32)
    for k in range(D_SGU // SGU_COL_CHUNK):
        sl = slice(k * SGU_COL_CHUNK, (k + 1) * SGU_COL_CHUNK)
        u_scr[:, sl] = _gelu(jnp.dot(h, win_ref[:, sl], preferred_element_type=F32))
        slv = slice(D_SGU + k * SGU_COL_CHUNK, D_SGU + (k + 1) * SGU_COL_CHUNK)
        zv = _gelu(jnp.dot(h, win_ref[:, slv], preferred_element_type=F32))
        ss = ss + jnp.sum(zv * zv, axis=-1, keepdims=True)
        v_scr[:, sl] = zv
    r = lax.rsqrt(ss * (1.0 / D_SGU) + EPS)

    ri = lax.broadcasted_iota(jnp.int32, (SGU_CHUNK, SGU_CHUNK), 0) // CHUNK
    ci = lax.broadcasted_iota(jnp.int32, (SGU_CHUNK, SGU_CHUNK), 1) // CHUNK
    mask = (ri == ci) if block_diag else (ri >= ci)
    for grp in range(SGU_GROUPS):
        gsl = slice(grp * SGU_GROUP_DIM, (grp + 1) * SGU_GROUP_DIM)
        ws = jnp.where(mask, ws_ref[grp], 0.0).astype(BF16)
        pieces = []
        for c in range(n_chunks):
            rsl = slice(c * SGU_CHUNK, (c + 1) * SGU_CHUNK)
            vn = v_scr[rsl, gsl] * r[rsl] * ng_ref[:, gsl]
            if emit_v:
                v_ref[rsl, gsl] = vn
            pieces.append(vn.astype(BF16))
        rhs = pieces[0] if n_chunks == 1 else jnp.concatenate(pieces, axis=1)
        mixed = jnp.dot(ws, rhs, preferred_element_type=F32) + b_ref[grp]
        for c in range(n_chunks):
            rsl = slice(c * SGU_CHUNK, (c + 1) * SGU_CHUNK)
            y = u_scr[rsl, gsl] * mixed[:, c * SGU_CHUNK:(c + 1) * SGU_CHUNK]
            y_scr[rsl, gsl] = y.astype(BF16)
    o_ref[...] = x + jnp.dot(y_scr[...], wout_ref[...], preferred_element_type=F32)


def _sgu_call(x, g, win, ng, ws, b, wout, layer, *, block_diag, emit_v, tm_want=512):
    rows = x.shape[0]
    tm = _row_tile(rows, tm_want)
    row_spec = pl.BlockSpec((tm, D_MODEL), lambda i: (i, 0))
    specs = [row_spec, _const_spec((1, D_MODEL)), _layer_spec(win, layer), _const_spec((1, D_SGU)),
             _layer_spec(ws, layer), _layer_spec(b, layer), _layer_spec(wout, layer)]
    out_shape = [jax.ShapeDtypeStruct((rows, D_MODEL), F32)]
    out_specs = [row_spec]
    if emit_v:
        out_shape.append(jax.ShapeDtypeStruct((rows, D_SGU), F32))
        out_specs.append(pl.BlockSpec((tm, D_SGU), lambda i: (i, 0)))
    res = pl.pallas_call(
        functools.partial(_sgu_kernel, block_diag=block_diag, emit_v=emit_v),
        grid=(rows // tm,), in_specs=specs, out_specs=out_specs, out_shape=out_shape,
        scratch_shapes=[pltpu.VMEM((tm, D_SGU), F32), pltpu.VMEM((tm, D_SGU), F32),
                        pltpu.VMEM((tm, D_SGU), BF16)],
        compiler_params=_params(1), name="sgu_mixer_v" if emit_v else "sgu_mixer")(
            x, g, win, ng, ws, b, wout)
    return res if emit_v else res[0]


ROPE_LANE0 = NOPE_DIM
ROPE_TAIL = HEAD_PAD - ROPE_LANE0 - ROPE_DIM
W_IN_PAD = Q_LORA + KV_LORA + 2 * LANES
Q_PAD = N_HEADS * HEAD_PAD
LOG2E = math.log2(math.e)


def _ones_lane(hd):
    return V_DIM if hd % 2 == 0 else 0


def _mla_proj_kernel(x_ref, g_ref, win_ref, qg_ref, kvg_ref, kcos_ref, ksin_ref, wuq_ref,
                     qtab_a_ref, qtab_b_ref, *rest, prompt_layout):
    if prompt_layout:
        wk_ref, wvt_ref, kg_ref = rest[:3]
        ckv_ref, kr_ref, q_ref, k_ref, vt_ref = rest[-5:]
    else:
        ckv_ref, kr_ref, krpad_ref, q_ref = rest[-4:]
    x = x_ref[...]
    h = _rms(x, g_ref[...], D_MODEL).astype(BF16)
    a = jnp.dot(h, win_ref[...], preferred_element_type=F32)
    c_q = _rms(a[:, :Q_LORA], qg_ref[...], Q_LORA).astype(BF16)
    c_kv = _rms(a[:, Q_LORA:Q_LORA + KV_LORA], kvg_ref[...], KV_LORA)
    ckv_ref[...] = c_kv

    kr = a[:, Q_LORA + KV_LORA:Q_LORA + KV_LORA + LANES]
    kr_sw = a[:, Q_LORA + KV_LORA + LANES:]
    r_kr = lax.rsqrt(jnp.sum(kr * kr, axis=-1, keepdims=True) * (1.0 / ROPE_DIM) + EPS)
    kr_rot = (kr * r_kr) * kcos_ref[...] + (kr_sw * r_kr) * ksin_ref[...]
    kr_ref[...] = kr_rot[:, ROPE_LANE0:ROPE_LANE0 + ROPE_DIM]

    if prompt_layout:
        c = c_kv.astype(BF16)
        _expand_keys(c, kr_rot, wk_ref, kg_ref, k_ref)
        _expand_values_t(c, wvt_ref, vt_ref)
        q_t = lax.dot_general(wuq_ref[...], c_q, (((1,), (1,)), ((), ())), preferred_element_type=F32)
        g_nope = qtab_a_ref[...]
        rot = qtab_b_ref[...]
        t_a, t_b, t_c, t_d = (rot[i * HALF_ROPE:(i + 1) * HALF_ROPE] for i in range(4))
        pad = jnp.zeros((ROPE_TAIL, q_t.shape[1]), F32)
        for hd in range(N_HEADS):
            base = hd * HEAD_PAD
            nope = q_t[base:base + NOPE_DIM]
            x1 = q_t[base + NOPE_DIM:base + NOPE_DIM + HALF_ROPE]
            x2 = q_t[base + NOPE_DIM + HALF_ROPE:base + QK_DIM]
            s_n = jnp.sum(nope * nope, axis=0, keepdims=True)
            s_r = jnp.sum(x1 * x1 + x2 * x2, axis=0, keepdims=True)
            r_n = lax.rsqrt(s_n * (1.0 / NOPE_DIM) + EPS)
            r_r = lax.rsqrt(s_r * (1.0 / ROPE_DIM) + EPS)
            x1, x2 = x1 * r_r, x2 * r_r
            q_ref[hd] = jnp.concatenate(
                [nope * r_n * g_nope, x1 * t_a - x2 * t_b, x2 * t_c + x1 * t_d, pad], axis=0).astype(BF16)
        return

    krpad_ref[...] = kr_rot.astype(BF16)
    qq = jnp.dot(c_q, wuq_ref[...], preferred_element_type=F32)
    qcos = qtab_a_ref[...]
    qsin = qtab_b_ref[...]
    is_nope = lax.broadcasted_iota(jnp.int32, (1, HEAD_PAD), 1) < NOPE_DIM
    for hd in range(N_HEADS):
        blk = qq[:, hd * HEAD_PAD:(hd + 1) * HEAD_PAD]
        blk_sw = qq[:, Q_PAD + hd * HEAD_PAD:Q_PAD + (hd + 1) * HEAD_PAD]
        sq = blk * blk
        s_n = jnp.sum(jnp.where(is_nope, sq, 0.0), axis=-1, keepdims=True)
        s_r = jnp.sum(jnp.where(is_nope, 0.0, sq), axis=-1, keepdims=True)
        r_n = lax.rsqrt(s_n * (1.0 / NOPE_DIM) + EPS)
        r_r = lax.rsqrt(s_r * (1.0 / ROPE_DIM) + EPS)
        q_ref[hd] = ((blk * jnp.where(is_nope, r_n, r_r)) * qcos + (blk_sw * r_r) * qsin).astype(BF16)


def _mla_proj_call(x, g, win, qg, kvg, k_tables, wuq, q_tables, layer, n_layers, prev=None, *,
                   kv_weights=None, tm_want=ATTN_BLOCK):
    prompt_layout = kv_weights is not None
    rows = x.shape[0]
    period = k_tables[0].shape[0]
    tm = _row_tile(min(rows, period), tm_want)
    n_per = period // tm
    row = lambda w: pl.BlockSpec((tm, w), lambda i: (i, 0))
    slot = lambda w: pl.BlockSpec((None, tm, w), lambda i: (layer, i, 0))
    tab = pl.BlockSpec((tm, HEAD_PAD), lambda i: (i % n_per, 0))
    head = pl.BlockSpec((N_HEADS, tm, HEAD_PAD), lambda i: (0, i, 0))
    head_shape = jax.ShapeDtypeStruct((N_HEADS, rows, HEAD_PAD), BF16)
    specs = [row(D_MODEL), _const_spec((1, D_MODEL)), _const_spec((D_MODEL, W_IN_PAD)),
             _const_spec((1, Q_LORA)), _const_spec((1, KV_LORA)), tab, tab, _const_spec(wuq.shape)]
    ins = [x, g, win, qg, kvg, *k_tables, wuq, *q_tables]
    out_shape = [jax.ShapeDtypeStruct((n_layers, rows, KV_LORA), F32),
                 jax.ShapeDtypeStruct((n_layers, rows, ROPE_DIM), F32)]
    out_specs = [slot(KV_LORA), slot(ROPE_DIM)]
    if prompt_layout:
        specs += [_const_spec((NOPE_DIM, tm)), pl.BlockSpec((2 * ROPE_DIM, tm), lambda i: (0, i % n_per))]
        specs += [_const_spec(w.shape) for w in kv_weights]
        ins += list(kv_weights)
        out_shape += [jax.ShapeDtypeStruct((N_HEADS, HEAD_PAD, rows), BF16), head_shape,
                      jax.ShapeDtypeStruct((N_HEADS, rows // tm, HEAD_PAD, tm), BF16)]
        out_specs += [pl.BlockSpec((N_HEADS, HEAD_PAD, tm), lambda i: (0, 0, i)), head,
                      pl.BlockSpec((N_HEADS, 1, HEAD_PAD, tm), lambda i: (0, i, 0, 0))]
    else:
        specs += [tab, tab]
        out_shape += [jax.ShapeDtypeStruct((rows, LANES), BF16), head_shape]
        out_specs += [row(LANES), head]
    aliases = {}
    if prev is not None:
        aliases = {len(ins): 0, len(ins) + 1: 1}
        ins += list(prev)
        specs += [pl.BlockSpec(memory_space=pl.ANY)] * 2
    return pl.pallas_call(
        functools.partial(_mla_proj_kernel, prompt_layout=prompt_layout), grid=(rows // tm,),
        in_specs=specs, out_specs=out_specs, out_shape=out_shape, input_output_aliases=aliases,
        compiler_params=_params(1), name="mla_project_kv" if prompt_layout else "mla_project")(*ins)


def _expand_keys(c, krp, wk_ref, kg_ref, k_dst):
    kk = jnp.dot(c, wk_ref[...], preferred_element_type=F32)
    for hd in range(N_HEADS):
        blk = kk[:, hd * HEAD_PAD:(hd + 1) * HEAD_PAD]
        r = lax.rsqrt(jnp.sum(blk * blk, axis=-1, keepdims=True) * (1.0 / NOPE_DIM) + EPS)
        k_dst[hd] = (blk * r * kg_ref[...] + krp).astype(BF16)


def _expand_values(c, wv_ref, v_dst):
    vv = jnp.dot(c, wv_ref[...], preferred_element_type=F32)
    lane = lax.broadcasted_iota(jnp.int32, (1, HEAD_PAD), 1)
    for hd in range(N_HEADS):
        vblk = vv[:, hd * HEAD_PAD:(hd + 1) * HEAD_PAD]
        v_dst[hd] = jnp.where(lane == _ones_lane(hd), 1.0, vblk).astype(BF16)


def _expand_values_t(c, wvt_ref, vt_dst):
    vt = lax.dot_general(wvt_ref[...], c, (((1,), (1,)), ((), ())), preferred_element_type=F32)
    row = lax.broadcasted_iota(jnp.int32, (HEAD_PAD, 1), 0)
    for hd in range(N_HEADS):
        blk = vt[hd * HEAD_PAD:(hd + 1) * HEAD_PAD, :]
        vt_dst[hd, 0] = jnp.where(row == _ones_lane(hd), 1.0, blk).astype(BF16)


def _kv_expand_kernel(c_ref, krpad_ref, wk_ref, wv_ref, kg_ref, k_ref, v_ref):
    c = c_ref[...].astype(BF16)
    _expand_keys(c, krpad_ref[...].astype(F32), wk_ref, kg_ref, k_ref)
    _expand_values(c, wv_ref, v_ref)


def _kv_expand_call(c_stack, layer, krpad, wk, wv, kg, tm_want=512):
    rows = c_stack.shape[1]
    tm = _row_tile(rows, tm_want)
    row = lambda w: pl.BlockSpec((tm, w), lambda i: (i, 0))
    c_spec = pl.BlockSpec((None, tm, KV_LORA), lambda i: (layer, i, 0))
    head = pl.BlockSpec((N_HEADS, tm, HEAD_PAD), lambda i: (0, i, 0))
    hs = jax.ShapeDtypeStruct((N_HEADS, rows, HEAD_PAD), BF16)
    return pl.pallas_call(
        _kv_expand_kernel, grid=(rows // tm,),
        in_specs=[c_spec, row(LANES), _const_spec(wk.shape), _const_spec(wv.shape),
                  _const_spec((1, HEAD_PAD))],
        out_specs=[head, head], out_shape=[hs, hs],
        compiler_params=_params(1), name="kv_expand")(c_stack, krpad, wk, wv, kg)


def _scores(q, k):
    batch_dims = tuple(range(q.ndim - 2))
    return lax.dot_general(q, k, (((q.ndim - 1,), (k.ndim - 1,)), (batch_dims, batch_dims)),
                           preferred_element_type=F32)


def _online_update(s, v, m, acc):
    m_new = jnp.maximum(m, jnp.max(s, axis=-1, keepdims=True))
    alpha = jnp.exp2(m - m_new)
    p = jnp.exp2(s - m_new).astype(BF16)
    batch_dims = tuple(range(p.ndim - 2))
    pv = lax.dot_general(p, v, (((p.ndim - 1,), (v.ndim - 2,)), (batch_dims, batch_dims)),
                         preferred_element_type=F32)
    return m_new, alpha * acc + pv


def _pair_output(acc_even, acc_odd):
    l_even = acc_even[:, _ones_lane(0):_ones_lane(0) + 1]
    l_odd = acc_odd[:, _ones_lane(1):_ones_lane(1) + 1]
    lane = lax.broadcasted_iota(jnp.int32, (1, HEAD_PAD), 1)
    return jnp.where(lane < V_DIM, acc_even / l_even, acc_odd / l_odd).astype(BF16)


HEADS_PER_STEP = 4
DIAG_STRIP = 256


def _prompt_attn_kernel(qt_ref, k_ref, vt_ref, o_ref, *, tq):
    seq = k_ref.shape[1]
    row = lax.broadcasted_iota(jnp.int32, (HEAD_PAD, 1), 0)
    heads = range(HEADS_PER_STEP)

    def update(s, vt, m, acc):
        m_new = jnp.maximum(m, jnp.max(s, axis=0, keepdims=True))
        alpha = jnp.exp2(m - m_new)
        p = jnp.exp2(s - m_new).astype(BF16)
        return m_new, alpha * acc + jnp.dot(vt, p, preferred_element_type=F32)

    n_strips = tq // DIAG_STRIP

    def scores(qi, step):
        q0 = qi * tq
        if step < qi:
            k_rows, q_cols = slice(step * tq, (step + 1) * tq), slice(q0, q0 + tq)
        else:
            c0 = (step - qi) * DIAG_STRIP
            k_rows, q_cols = slice(q0, q0 + c0 + DIAG_STRIP), slice(q0 + c0, q0 + c0 + DIAG_STRIP)
        return [jnp.dot(k_ref[hd, k_rows, :], qt_ref[hd, :, q_cols], preferred_element_type=F32)
                for hd in heads]

    for qi in range(seq // tq):
        q0 = qi * tq
        carry = [(jnp.full((1, tq), NEG_INF, F32), jnp.zeros((HEAD_PAD, tq), F32)) for _ in heads]
        n_steps = qi + n_strips
        ss = scores(qi, 0)
        for step in range(n_steps):
            ss_next = scores(qi, step + 1) if step + 1 < n_steps else None
            if step < qi:
                carry = [update(s, vt_ref[hd, step], m, acc) for s, hd, (m, acc) in zip(ss, heads, carry)]
            else:
                c0 = (step - qi) * DIAG_STRIP
                n_keys = c0 + DIAG_STRIP
                key_chunk = lax.broadcasted_iota(jnp.int32, (n_keys, DIAG_STRIP), 0) // CHUNK
                qry_chunk = (lax.broadcasted_iota(jnp.int32, (n_keys, DIAG_STRIP), 1) + c0) // CHUNK
                mask = key_chunk <= qry_chunk
                accs = [update(jnp.where(mask, s, NEG_INF), vt_ref[hd, qi, :, :n_keys],
                               m[:, c0:c0 + DIAG_STRIP], acc[:, c0:c0 + DIAG_STRIP])[1]
                        for s, hd, (m, acc) in zip(ss, heads, carry)]
                for p in range(HEADS_PER_STEP // 2):
                    acc_e, acc_o = accs[2 * p], accs[2 * p + 1]
                    l_e = acc_e[_ones_lane(0):_ones_lane(0) + 1, :]
                    l_o = acc_o[_ones_lane(1):_ones_lane(1) + 1, :]
                    o_t = jnp.where(row < V_DIM, acc_e / l_e, acc_o / l_o)
                    o_ref[q0 + c0:q0 + n_keys, p * LANES:(p + 1) * LANES] = o_t.T.astype(BF16)
            ss = ss_next


def _prompt_attn_call(q_t, k, vt, batch, seq, tq=ATTN_BLOCK):
    tq = min(tq, seq)
    nq = seq // tq
    rows = batch * seq
    assert vt.shape == (N_HEADS, batch * nq, HEAD_PAD, tq)
    g = HEADS_PER_STEP
    return pl.pallas_call(
        functools.partial(_prompt_attn_kernel, tq=tq),
        grid=(batch, N_HEADS // g),
        in_specs=[pl.BlockSpec((g, HEAD_PAD, seq), lambda b, i: (i, 0, b)),
                  pl.BlockSpec((g, seq, HEAD_PAD), lambda b, i: (i, b, 0)),
                  pl.BlockSpec((g, nq, HEAD_PAD, tq), lambda b, i: (i, b, 0, 0))],
        out_specs=pl.BlockSpec((seq, g // 2 * LANES), lambda b, i: (b, i)),
        out_shape=jax.ShapeDtypeStruct((rows, D_MODEL), BF16),
        compiler_params=_params(2), name="prompt_attention")(q_t, k, vt)


def _sample_attn_kernel(q_ref, c_ref, kr_ref, wkt_ref, wv_ref, kgt_ref, kn_ref, vn_ref, o_ref,
                        kt_scr, v_scr, m_scr, acc_scr):
    kb = pl.program_id(1)
    last = pl.num_programs(1) - 1

    @pl.when(kb == 0)
    def _():
        m_scr[...] = jnp.full(m_scr.shape, NEG_INF, F32)
        acc_scr[...] = jnp.zeros(acc_scr.shape, F32)

    place_t = (lax.broadcasted_iota(jnp.int32, (HEAD_PAD, ROPE_DIM), 0)
               == lax.broadcasted_iota(jnp.int32, (HEAD_PAD, ROPE_DIM), 1) + ROPE_LANE0).astype(BF16)
    contract_last = (((1,), (1,)), ((), ()))
    rope_rows = lax.dot_general(place_t, kr_ref[...].astype(BF16), contract_last,
                                preferred_element_type=F32)[NOPE_DIM:]
    c = c_ref[...].astype(BF16)
    k_t = lax.dot_general(wkt_ref[...], c, contract_last, preferred_element_type=F32)
    for hd in range(N_HEADS):
        blk = k_t[hd * NOPE_DIM:(hd + 1) * NOPE_DIM]
        r = lax.rsqrt(jnp.sum(blk * blk, axis=0, keepdims=True) * (1.0 / NOPE_DIM) + EPS)
        kt_scr[hd] = jnp.concatenate([blk * r * kgt_ref[...], rope_rows], axis=0).astype(BF16)
    _expand_values(c, wv_ref, v_scr)

    q = q_ref[...]
    s = lax.dot_general(q, kt_scr[...], (((2,), (1,)), ((0,), (0,))), preferred_element_type=F32)
    m, acc = _online_update(s, v_scr[...], m_scr[...], acc_scr[...])
    m_scr[...] = m
    acc_scr[...] = acc

    @pl.when(kb == last)
    def _():
        _, acc_f = _online_update(_scores(q, kn_ref[...]), vn_ref[...], m, acc)
        for pair in range(N_HEADS // 2):
            o_ref[:, pair * LANES:(pair + 1) * LANES] = _pair_output(acc_f[2 * pair], acc_f[2 * pair + 1])


def _sample_attn_call(q, cache_c, cache_kr, layer, wk_t, wv, kg_t, kn, vn, batch, n_new, past, tk=512):
    tk = min(tk, past)
    nk = past // tk
    new_spec = pl.BlockSpec((N_HEADS, n_new, HEAD_PAD), lambda b, j: (0, b, 0))
    cache = lambda w: pl.BlockSpec((None, None, tk, w), lambda b, j: (layer, b, j, 0))
    return pl.pallas_call(
        _sample_attn_kernel, grid=(batch, nk),
        in_specs=[new_spec, cache(KV_LORA), cache(ROPE_DIM), _const_spec(wk_t.shape),
                  _const_spec(wv.shape), _const_spec((NOPE_DIM, tk)), new_spec, new_spec],
        out_specs=pl.BlockSpec((n_new, D_MODEL), lambda b, j: (b, 0)),
        out_shape=jax.ShapeDtypeStruct((batch * n_new, D_MODEL), BF16),
        scratch_shapes=[pltpu.VMEM((N_HEADS, HEAD_PAD, tk), BF16),
                        pltpu.VMEM((N_HEADS, tk, HEAD_PAD), BF16),
                        pltpu.VMEM((N_HEADS, n_new, 1), F32),
                        pltpu.VMEM((N_HEADS, n_new, HEAD_PAD), F32)],
        compiler_params=_params(2), name="sample_attention")(
            q, cache_c, cache_kr, wk_t, wv, jnp.broadcast_to(kg_t, (NOPE_DIM, tk)), kn, vn)


def _lane_vec(parts):
    v = jnp.concatenate(parts)
    return jnp.pad(v, (0, LANES - v.shape[0])).reshape(1, LANES)


def _rope_tables(pos):
    inv = ROPE_BASE ** (-jnp.arange(HALF_ROPE, dtype=F32) / HALF_ROPE)
    ang = pos.astype(F32)[:, None] * inv[None, :]
    cos, sin = jnp.cos(ang), jnp.sin(ang)
    n = pos.shape[0]
    zeros = lambda w: jnp.zeros((n, w), F32)
    cos_t = jnp.concatenate([jnp.ones((n, NOPE_DIM), F32), cos, cos, zeros(ROPE_TAIL)], axis=1)
    sin_t = jnp.concatenate([zeros(ROPE_LANE0), -sin, sin, zeros(ROPE_TAIL)], axis=1)
    return cos_t, sin_t, cos.T, sin.T


def _prep_mla(w_in, w_uq, w_ukv, q_nope_g, q_rope_g, k_nope_g, k_rope_g):
    zeros = lambda n: jnp.zeros((n,), F32)
    swap = lambda g: jnp.concatenate([g[HALF_ROPE:], g[:HALF_ROPE]])
    w_q, w_kv = w_in[:, :Q_LORA], w_in[:, Q_LORA:Q_LORA + KV_LORA]
    w_kr1 = w_in[:, Q_LORA + KV_LORA:Q_LORA + KV_LORA + HALF_ROPE]
    w_kr2 = w_in[:, Q_LORA + KV_LORA + HALF_ROPE:]
    zc = lambda n: jnp.zeros((D_MODEL, n), F32)
    win = jnp.concatenate([w_q, w_kv, zc(ROPE_LANE0), w_kr1, w_kr2, zc(ROPE_TAIL),
                           zc(ROPE_LANE0), w_kr2, w_kr1, zc(ROPE_TAIL)], axis=1).astype(BF16)
    gains = dict(krg=_lane_vec([zeros(ROPE_LANE0), k_rope_g]),
                 krgs=_lane_vec([zeros(ROPE_LANE0), swap(k_rope_g)]),
                 qhg=_lane_vec([q_nope_g, q_rope_g]) * (ATTN_SCALE * LOG2E),
                 qhgs=_lane_vec([zeros(ROPE_LANE0), swap(q_rope_g)]) * (ATTN_SCALE * LOG2E))

    uq = w_uq.reshape(Q_LORA, N_HEADS, QK_DIM)
    uq_n, uq_1, uq_2 = uq[..., :NOPE_DIM], uq[..., NOPE_DIM:NOPE_DIM + HALF_ROPE], uq[..., NOPE_DIM + HALF_ROPE:]
    zq = lambda n: jnp.zeros((Q_LORA, N_HEADS, n), F32)
    main = jnp.concatenate([uq_n, uq_1, uq_2, zq(ROPE_TAIL)], axis=-1).reshape(Q_LORA, Q_PAD)
    swapped = jnp.concatenate([zq(ROPE_LANE0), uq_2, uq_1, zq(ROPE_TAIL)], axis=-1).reshape(Q_LORA, Q_PAD)
    wuq = jnp.concatenate([main, swapped], axis=1).astype(BF16)
    wuq_t = main.T.astype(BF16)

    ukv = w_ukv.reshape(KV_LORA, N_HEADS // 2, 2, NOPE_DIM + V_DIM)
    zk = jnp.zeros((KV_LORA, N_HEADS // 2, 2, HEAD_PAD - NOPE_DIM), F32)
    wk = jnp.concatenate([ukv[..., :NOPE_DIM], zk], axis=-1).reshape(KV_LORA, Q_PAD).astype(BF16)
    wk_t = ukv[..., :NOPE_DIM].reshape(KV_LORA, N_HEADS * NOPE_DIM).T.astype(BF16)
    zv = jnp.zeros((KV_LORA, N_HEADS // 2, V_DIM), F32)
    v_even = jnp.concatenate([ukv[:, :, 0, NOPE_DIM:], zv], axis=-1)
    v_odd = jnp.concatenate([zv, ukv[:, :, 1, NOPE_DIM:]], axis=-1)
    wv = jnp.stack([v_even, v_odd], axis=2).reshape(KV_LORA, Q_PAD).astype(BF16)
    gains.update(q_nope=q_nope_g * (ATTN_SCALE * LOG2E), q_rope=q_rope_g * (ATTN_SCALE * LOG2E))
    gains.update(k_nope_col=k_nope_g[:, None])
    return win, wuq, wuq_t, wk, wk_t, wv, _lane_vec([k_nope_g]), gains


def _key_tables(cos_t, sin_t, gains):
    return cos_t * gains["krg"], sin_t * gains["krgs"]


def _query_tables(cos_t, sin_t, gains):
    return cos_t * gains["qhg"], sin_t * gains["qhgs"]


def _query_tables_t(cos_tt, sin_tt, gains, tm):
    g1, g2 = gains["q_rope"][:HALF_ROPE, None], gains["q_rope"][HALF_ROPE:, None]
    rot = jnp.concatenate([g1 * cos_tt, g2 * sin_tt, g2 * cos_tt, g1 * sin_tt], axis=0)
    return jnp.broadcast_to(gains["q_nope"][:, None], (NOPE_DIM, tm)), rot


def kernel(x_prompt, x_sample, cache_kv_latent, cache_k_rope, mix_norm_g, mlp_norm_g, sgu_w_in, sgu_norm_g, sgu_w_s, sgu_b_s, sgu_w_out, mla_w_in, mla_q_norm_g, mla_kv_norm_g, mla_w_uq, mla_w_ukv, mla_q_nope_g, mla_q_rope_g, mla_k_nope_g, mla_k_rope_g, mla_w_o, mlp_w_up, mlp_w_down):
    batch, seq, _ = x_prompt.shape
    dec_batch, n_new, _ = x_sample.shape
    past = cache_kv_latent.shape[2]
    depth = mix_norm_g.shape[0]
    rows_p, rows_s = batch * seq, dec_batch * n_new

    xp = x_prompt.reshape(rows_p, D_MODEL)
    xs = x_sample.reshape(rows_s, D_MODEL)
    cos_p, sin_p, cos_pt, sin_pt = _rope_tables(jnp.arange(seq))
    cos_s, sin_s, _, _ = _rope_tables(past + jnp.arange(n_new))
    cos_s, sin_s = jnp.tile(cos_s, (dec_batch, 1)), jnp.tile(sin_s, (dec_batch, 1))
    mlp_wu, mlp_wd = mlp_w_up.astype(BF16), mlp_w_down.astype(BF16)
    sgu_win, sgu_wout = sgu_w_in.astype(BF16), sgu_w_out.astype(BF16)
    mla_wo = mla_w_o.astype(BF16)
    reps = SGU_CHUNK // n_new
    sgu_ws_s = jnp.tile(sgu_w_s[:, :, :n_new, :n_new], (1, 1, reps, reps))
    sgu_b_p = sgu_b_s[..., None]
    sgu_b_smp = jnp.tile(sgu_b_s[:, :, :n_new], (1, 1, reps))[..., None]

    n_mla = mla_w_in.shape[0]
    cache_p = cache_s = None
    sgu_v_s = []
    for i in range(depth):
        j = i // 2
        g_mix = mix_norm_g[i].reshape(1, D_MODEL)
        g_mlp = mlp_norm_g[i].reshape(1, D_MODEL)
        if i % 2 == 0:
            ng = sgu_norm_g[j].reshape(1, D_SGU)
            xp = _sgu_call(xp, g_mix, sgu_win, ng, sgu_w_s, sgu_b_p, sgu_wout, j,
                           block_diag=False, emit_v=False)
            xs, v_new = _sgu_call(xs, g_mix, sgu_win, ng, sgu_ws_s, sgu_b_smp, sgu_wout, j,
                                  block_diag=True, emit_v=True)
            sgu_v_s.append(v_new.reshape(dec_batch, n_new, D_SGU))
            xp = _mlp_call(xp, g_mlp, mlp_wu, mlp_wd, i)
            xs = _mlp_call(xs, g_mlp, mlp_wu, mlp_wd, i)
        else:
            win, wuq, wuq_t, wk, wk_t, wv, kg, gains = _prep_mla(
                mla_w_in[j], mla_w_uq[j], mla_w_ukv[j], mla_q_nope_g[j], mla_q_rope_g[j],
                mla_k_nope_g[j], mla_k_rope_g[j])
            qg, kvg = mla_q_norm_g[j].reshape(1, -1), mla_kv_norm_g[j].reshape(1, -1)
            *cache_p, q_t, k, vt = _mla_proj_call(
                xp, g_mix, win, qg, kvg, _key_tables(cos_p, sin_p, gains), wuq_t,
                _query_tables_t(cos_pt, sin_pt, gains, min(ATTN_BLOCK, seq)), j, n_mla, cache_p,
                kv_weights=(wk, wv.T, kg))
            attn = _prompt_attn_call(q_t, k, vt, batch, seq)
            xp = _mlp_call(xp, g_mlp, mlp_wu, mlp_wd, i, attn=attn, wo=mla_wo, wo_layer=j)
            *cache_s, krpad, q = _mla_proj_call(
                xs, g_mix, win, qg, kvg, _key_tables(cos_s, sin_s, gains), wuq,
                _query_tables(cos_s, sin_s, gains), j, n_mla, cache_s)
            kn, vn = _kv_expand_call(cache_s[0], j, krpad, wk, wv, kg)
            attn = _sample_attn_call(q, cache_kv_latent, cache_k_rope, j, wk_t, wv, gains["k_nope_col"],
                                     kn, vn, dec_batch, n_new, past)
            xs = _mlp_call(xs, g_mlp, mlp_wu, mlp_wd, i, attn=attn, wo=mla_wo, wo_layer=j)
    return (xp.reshape(batch, seq, D_MODEL), xs.reshape(dec_batch, n_new, D_MODEL),
            cache_p[0].reshape(n_mla, batch, seq, KV_LORA), cache_p[1].reshape(n_mla, batch, seq, ROPE_DIM),
            cache_s[0].reshape(n_mla, dec_batch, n_new, KV_LORA),
            cache_s[1].reshape(n_mla, dec_batch, n_new, ROPE_DIM), jnp.stack(sgu_v_s))
```

```python
import functools
import math

import jax
import jax.numpy as jnp
import numpy as np
from jax import lax
from jax.experimental import pallas as pl
from jax.experimental.pallas import tpu as pltpu

F32 = jnp.float32
BF16 = jnp.bfloat16

D_MODEL = 1024
CHUNK = 64
SGU_CHUNK = 128
D_SGU = 2 * D_MODEL
SGU_GROUP_DIM = 128
SGU_GROUPS = D_SGU // SGU_GROUP_DIM
N_HEADS = 16
Q_LORA = 384
KV_LORA = 256
NOPE_DIM = 64
ROPE_DIM = 32
HALF_ROPE = ROPE_DIM // 2
V_DIM = 64
QK_DIM = NOPE_DIM + ROPE_DIM
ATTN_SCALE = 1.0 / math.sqrt(QK_DIM)
ROPE_BASE = 10000.0
D_FF = 4 * D_MODEL
EPS = 1e-6
NEG_INF = -1e30

LANES = 128
HEAD_PAD = LANES
FF_CHUNK = 1024
SGU_COL_CHUNK = 512
VMEM_LIMIT = 56 * 1024 * 1024
ATTN_BLOCK = 512


def _params(n_axes):
    return pltpu.CompilerParams(
        dimension_semantics=("arbitrary",) * n_axes, vmem_limit_bytes=VMEM_LIMIT)


def _const_spec(shape):
    nd = len(shape)
    return pl.BlockSpec(shape, lambda *_: (0,) * nd, pipeline_mode=pl.Buffered(1))


def _layer_spec(stack, layer):
    shape = stack.shape[1:]
    return pl.BlockSpec((None,) + shape, lambda *_: (layer,) + (0,) * len(shape),
                        pipeline_mode=pl.Buffered(1))


def _row_tile(rows, want):
    tm = min(rows, want)
    assert rows % tm == 0
    return tm


def _rms(xf, g, n):
    ss = jnp.sum(xf * xf, axis=-1, keepdims=True)
    return xf * lax.rsqrt(ss * (1.0 / n) + EPS) * g


def _gelu(z):
    return 0.5 * z * (1.0 + lax.erf(z * np.float32(math.sqrt(0.5))))


def _mlp_delta(x, g_ref, wu_ref, wd_ref):
    h = _rms(x, g_ref[...], D_MODEL).astype(BF16)
    acc = None
    for f in range(D_FF // FF_CHUNK):
        sl = slice(f * FF_CHUNK, (f + 1) * FF_CHUNK)
        a = jnp.dot(h, wu_ref[:, sl], preferred_element_type=F32)
        a = jnp.square(jnp.maximum(a, 0.0)).astype(BF16)
        d = jnp.dot(a, wd_ref[sl, :], preferred_element_type=F32)
        acc = d if acc is None else acc + d
    return acc


def _mlp_kernel(x_ref, g_ref, wu_ref, wd_ref, o_ref):
    x = x_ref[...]
    o_ref[...] = x + _mlp_delta(x, g_ref, wu_ref, wd_ref)


def _attn_out_mlp_kernel(x_ref, a_ref, wo_ref, g_ref, wu_ref, wd_ref, o_ref):
    x = x_ref[...] + jnp.dot(a_ref[...], wo_ref[...], preferred_element_type=F32)
    o_ref[...] = x + _mlp_delta(x, g_ref, wu_ref, wd_ref)


def _mlp_call(x, g, wu, wd, layer, attn=None, wo=None, wo_layer=None, tm_want=512):
    rows = x.shape[0]
    tm = _row_tile(rows, tm_want)
    row_spec = pl.BlockSpec((tm, D_MODEL), lambda i: (i, 0))
    w_specs = [_const_spec((1, D_MODEL)), _layer_spec(wu, layer), _layer_spec(wd, layer)]
    if attn is None:
        kern, ins, specs = _mlp_kernel, (x, g, wu, wd), [row_spec] + w_specs
        name = "channel_mlp"
    else:
        kern, ins = _attn_out_mlp_kernel, (x, attn, wo, g, wu, wd)
        specs = [row_spec, row_spec, _layer_spec(wo, wo_layer)] + w_specs
        name = "attn_out_channel_mlp"
    return pl.pallas_call(
        kern, grid=(rows // tm,), in_specs=specs, out_specs=row_spec,
        out_shape=jax.ShapeDtypeStruct((rows, D_MODEL), F32),
        compiler_params=_params(1), name=name)(*ins)


def _sgu_kernel(x_ref, g_ref, win_ref, ng_ref, ws_ref, b_ref, wout_ref, *rest, block_diag, emit_v):
    if emit_v:
        o_ref, v_ref, u_scr, v_scr, y_scr = rest
    else:
        o_ref, u_scr, v_scr, y_scr = rest
        v_ref = None
    tm = x_ref.shape[0]
    n_chunks = tm // SGU_CHUNK
    x = x_ref[...]
    h = _rms(x, g_ref[...], D_MODEL).astype(BF16)

    ss = jnp.zeros((tm, 1), F32)
    for k in range(D_SGU // SGU_COL_CHUNK):
        sl = slice(k * SGU_COL_CHUNK, (k + 1) * SGU_COL_CHUNK)
        u_scr[:, sl] = _gelu(jnp.dot(h, win_ref[:, sl], preferred_element_type=F32))
        slv = slice(D_SGU + k * SGU_COL_CHUNK, D_SGU + (k + 1) * SGU_COL_CHUNK)
        zv = _gelu(jnp.dot(h, win_ref[:, slv], preferred_element_type=F32))
        ss = ss + jnp.sum(zv * zv, axis=-1, keepdims=True)
        v_scr[:, sl] = zv
    r = lax.rsqrt(ss * (1.0 / D_SGU) + EPS)

    ri = lax.broadcasted_iota(jnp.int32, (SGU_CHUNK, SGU_CHUNK), 0) // CHUNK
    ci = lax.broadcasted_iota(jnp.int32, (SGU_CHUNK, SGU_CHUNK), 1) // CHUNK
    mask = (ri == ci) if block_diag else (ri >= ci)
    for grp in range(SGU_GROUPS):
        gsl = slice(grp * SGU_GROUP_DIM, (grp + 1) * SGU_GROUP_DIM)
        ws = jnp.where(mask, ws_ref[grp], 0.0).astype(BF16)
        pieces = []
        for c in range(n_chunks):
            rsl = slice(c * SGU_CHUNK, (c + 1) * SGU_CHUNK)
            vn = v_scr[rsl, gsl] * r[rsl] * ng_ref[:, gsl]
            if emit_v:
                v_ref[rsl, gsl] = vn
            pieces.append(vn.astype(BF16))
        rhs = pieces[0] if n_chunks == 1 else jnp.concatenate(pieces, axis=1)
        mixed = jnp.dot(ws, rhs, preferred_element_type=F32) + b_ref[grp]
        for c in range(n_chunks):
            rsl = slice(c * SGU_CHUNK, (c + 1) * SGU_CHUNK)
            y = u_scr[rsl, gsl] * mixed[:, c * SGU_CHUNK:(c + 1) * SGU_CHUNK]
            y_scr[rsl, gsl] = y.astype(BF16)
    o_ref[...] = x + jnp.dot(y_scr[...], wout_ref[...], preferred_element_type=F32)


def _sgu_call(x, g, win, ng, ws, b, wout, layer, *, block_diag, emit_v, tm_want=512):
    rows = x.shape[0]
    tm = _row_tile(rows, tm_want)
    row_spec = pl.BlockSpec((tm, D_MODEL), lambda i: (i, 0))
    specs = [row_spec, _const_spec((1, D_MODEL)), _layer_spec(win, layer), _const_spec((1, D_SGU)),
             _layer_spec(ws, layer), _layer_spec(b, layer), _layer_spec(wout, layer)]
    out_shape = [jax.ShapeDtypeStruct((rows, D_MODEL), F32)]
    out_specs = [row_spec]
    if emit_v:
        out_shape.append(jax.ShapeDtypeStruct((rows, D_SGU), F32))
        out_specs.append(pl.BlockSpec((tm, D_SGU), lambda i: (i, 0)))
    res = pl.pallas_call(
        functools.partial(_sgu_kernel, block_diag=block_diag, emit_v=emit_v),
        grid=(rows // tm,), in_specs=specs, out_specs=out_specs, out_shape=out_shape,
        scratch_shapes=[pltpu.VMEM((tm, D_SGU), F32), pltpu.VMEM((tm, D_SGU), F32),
                        pltpu.VMEM((tm, D_SGU), BF16)],
        compiler_params=_params(1), name="sgu_mixer_v" if emit_v else "sgu_mixer")(
            x, g, win, ng, ws, b, wout)
    return res if emit_v else res[0]


ROPE_LANE0 = NOPE_DIM
ROPE_TAIL = HEAD_PAD - ROPE_LANE0 - ROPE_DIM
W_IN_PAD = Q_LORA + KV_LORA + 2 * LANES
Q_PAD = N_HEADS * HEAD_PAD
LOG2E = math.log2(math.e)


def _ones_lane(hd):
    return V_DIM if hd % 2 == 0 else 0


def _mla_proj_kernel(x_ref, g_ref, win_ref, qg_ref, kvg_ref, kcos_ref, ksin_ref, wuq_ref,
                     qtab_a_ref, qtab_b_ref, *rest, prompt_layout):
    if prompt_layout:
        wk_ref, wvt_ref, kg_ref = rest[:3]
        ckv_ref, kr_ref, q_ref, k_ref, vt_ref = rest[-5:]
    else:
        ckv_ref, kr_ref, krpad_ref, q_ref = rest[-4:]
    x = x_ref[...]
    h = _rms(x, g_ref[...], D_MODEL).astype(BF16)
    a = jnp.dot(h, win_ref[...], preferred_element_type=F32)
    c_q = _rms(a[:, :Q_LORA], qg_ref[...], Q_LORA).astype(BF16)
    c_kv = _rms(a[:, Q_LORA:Q_LORA + KV_LORA], kvg_ref[...], KV_LORA)
    ckv_ref[...] = c_kv

    kr = a[:, Q_LORA + KV_LORA:Q_LORA + KV_LORA + LANES]
    kr_sw = a[:, Q_LORA + KV_LORA + LANES:]
    r_kr = lax.rsqrt(jnp.sum(kr * kr, axis=-1, keepdims=True) * (1.0 / ROPE_DIM) + EPS)
    kr_rot = (kr * r_kr) * kcos_ref[...] + (kr_sw * r_kr) * ksin_ref[...]
    kr_ref[...] = kr_rot[:, ROPE_LANE0:ROPE_LANE0 + ROPE_DIM]

    if prompt_layout:
        c = c_kv.astype(BF16)
        _expand_keys(c, kr_rot, wk_ref, kg_ref, k_ref)
        _expand_values_t(c, wvt_ref, vt_ref)
        q_t = lax.dot_general(wuq_ref[...], c_q, (((1,), (1,)), ((), ())), preferred_element_type=F32)
        g_nope = qtab_a_ref[...]
        rot = qtab_b_ref[...]
        t_a, t_b, t_c, t_d = (rot[i * HALF_ROPE:(i + 1) * HALF_ROPE] for i in range(4))
        pad = jnp.zeros((ROPE_TAIL, q_t.shape[1]), F32)
        for hd in range(N_HEADS):
            base = hd * HEAD_PAD
            nope = q_t[base:base + NOPE_DIM]
            x1 = q_t[base + NOPE_DIM:base + NOPE_DIM + HALF_ROPE]
            x2 = q_t[base + NOPE_DIM + HALF_ROPE:base + QK_DIM]
            s_n = jnp.sum(nope * nope, axis=0, keepdims=True)
            s_r = jnp.sum(x1 * x1 + x2 * x2, axis=0, keepdims=True)
            r_n = lax.rsqrt(s_n * (1.0 / NOPE_DIM) + EPS)
            r_r = lax.rsqrt(s_r * (1.0 / ROPE_DIM) + EPS)
            x1, x2 = x1 * r_r, x2 * r_r
            q_ref[hd] = jnp.concatenate(
                [nope * r_n * g_nope, x1 * t_a - x2 * t_b, x2 * t_c + x1 * t_d, pad], axis=0).astype(BF16)
        return

    krpad_ref[...] = kr_rot.astype(BF16)
    qq = jnp.dot(c_q, wuq_ref[...], preferred_element_type=F32)
    qcos = qtab_a_ref[...]
    qsin = qtab_b_ref[...]
    is_nope = lax.broadcasted_iota(jnp.int32, (1, HEAD_PAD), 1) < NOPE_DIM
    for hd in range(N_HEADS):
        blk = qq[:, hd * HEAD_PAD:(hd + 1) * HEAD_PAD]
        blk_sw = qq[:, Q_PAD + hd * HEAD_PAD:Q_PAD + (hd + 1) * HEAD_PAD]
        sq = blk * blk
        s_n = jnp.sum(jnp.where(is_nope, sq, 0.0), axis=-1, keepdims=True)
        s_r = jnp.sum(jnp.where(is_nope, 0.0, sq), axis=-1, keepdims=True)
        r_n = lax.rsqrt(s_n * (1.0 / NOPE_DIM) + EPS)
        r_r = lax.rsqrt(s_r * (1.0 / ROPE_DIM) + EPS)
        q_ref[hd] = ((blk * jnp.where(is_nope, r_n, r_r)) * qcos + (blk_sw * r_r) * qsin).astype(BF16)


def _mla_proj_call(x, g, win, qg, kvg, k_tables, wuq, q_tables, layer, n_layers, prev=None, *,
                   kv_weights=None, tm_want=ATTN_BLOCK):
    prompt_layout = kv_weights is not None
    rows = x.shape[0]
    period = k_tables[0].shape[0]
    tm = _row_tile(min(rows, period), tm_want)
    n_per = period // tm
    row = lambda w: pl.BlockSpec((tm, w), lambda i: (i, 0))
    slot = lambda w: pl.BlockSpec((None, tm, w), lambda i: (layer, i, 0))
    tab = pl.BlockSpec((tm, HEAD_PAD), lambda i: (i % n_per, 0))
    head = pl.BlockSpec((N_HEADS, tm, HEAD_PAD), lambda i: (0, i, 0))
    head_shape = jax.ShapeDtypeStruct((N_HEADS, rows, HEAD_PAD), BF16)
    specs = [row(D_MODEL), _const_spec((1, D_MODEL)), _const_spec((D_MODEL, W_IN_PAD)),
             _const_spec((1, Q_LORA)), _const_spec((1, KV_LORA)), tab, tab, _const_spec(wuq.shape)]
    ins = [x, g, win, qg, kvg, *k_tables, wuq, *q_tables]
    out_shape = [jax.ShapeDtypeStruct((n_layers, rows, KV_LORA), F32),
                 jax.ShapeDtypeStruct((n_layers, rows, ROPE_DIM), F32)]
    out_specs = [slot(KV_LORA), slot(ROPE_DIM)]
    if prompt_layout:
        specs += [_const_spec((NOPE_DIM, tm)), pl.BlockSpec((2 * ROPE_DIM, tm), lambda i: (0, i % n_per))]
        specs += [_const_spec(w.shape) for w in kv_weights]
        ins += list(kv_weights)
        out_shape += [jax.ShapeDtypeStruct((N_HEADS, HEAD_PAD, rows), BF16), head_shape,
                      jax.ShapeDtypeStruct((N_HEADS, rows // tm, HEAD_PAD, tm), BF16)]
        out_specs += [pl.BlockSpec((N_HEADS, HEAD_PAD, tm), lambda i: (0, 0, i)), head,
                      pl.BlockSpec((N_HEADS, 1, HEAD_PAD, tm), lambda i: (0, i, 0, 0))]
    else:
        specs += [tab, tab]
        out_shape += [jax.ShapeDtypeStruct((rows, LANES), BF16), head_shape]
        out_specs += [row(LANES), head]
    aliases = {}
    if prev is not None:
        aliases = {len(ins): 0, len(ins) + 1: 1}
        ins += list(prev)
        specs += [pl.BlockSpec(memory_space=pl.ANY)] * 2
    return pl.pallas_call(
        functools.partial(_mla_proj_kernel, prompt_layout=prompt_layout), grid=(rows // tm,),
        in_specs=specs, out_specs=out_specs, out_shape=out_shape, input_output_aliases=aliases,
        compiler_params=_params(1), name="mla_project_kv" if prompt_layout else "mla_project")(*ins)


def _expand_keys(c, krp, wk_ref, kg_ref, k_dst):
    kk = jnp.dot(c, wk_ref[...], preferred_element_type=F32)
    for hd in range(N_HEADS):
        blk = kk[:, hd * HEAD_PAD:(hd + 1) * HEAD_PAD]
        r = lax.rsqrt(jnp.sum(blk * blk, axis=-1, keepdims=True) * (1.0 / NOPE_DIM) + EPS)
        k_dst[hd] = (blk * r * kg_ref[...] + krp).astype(BF16)


def _expand_values(c, wv_ref, v_dst):
    vv = jnp.dot(c, wv_ref[...], preferred_element_type=F32)
    lane = lax.broadcasted_iota(jnp.int32, (1, HEAD_PAD), 1)
    for hd in range(N_HEADS):
        vblk = vv[:, hd * HEAD_PAD:(hd + 1) * HEAD_PAD]
        v_dst[hd] = jnp.where(lane == _ones_lane(hd), 1.0, vblk).astype(BF16)


def _expand_values_t(c, wvt_ref, vt_dst):
    vt = lax.dot_general(wvt_ref[...], c, (((1,), (1,)), ((), ())), preferred_element_type=F32)
    row = lax.broadcasted_iota(jnp.int32, (HEAD_PAD, 1), 0)
    for hd in range(N_HEADS):
        blk = vt[hd * HEAD_PAD:(hd + 1) * HEAD_PAD, :]
        vt_dst[hd, 0] = jnp.where(row == _ones_lane(hd), 1.0, blk).astype(BF16)


def _kv_expand_kernel(c_ref, krpad_ref, wk_ref, wv_ref, kg_ref, k_ref, v_ref):
    c = c_ref[...].astype(BF16)
    _expand_keys(c, krpad_ref[...].astype(F32), wk_ref, kg_ref, k_ref)
    _expand_values(c, wv_ref, v_ref)


def _kv_expand_call(c_stack, layer, krpad, wk, wv, kg, tm_want=512):
    rows = c_stack.shape[1]
    tm = _row_tile(rows, tm_want)
    row = lambda w: pl.BlockSpec((tm, w), lambda i: (i, 0))
    c_spec = pl.BlockSpec((None, tm, KV_LORA), lambda i: (layer, i, 0))
    head = pl.BlockSpec((N_HEADS, tm, HEAD_PAD), lambda i: (0, i, 0))
    hs = jax.ShapeDtypeStruct((N_HEADS, rows, HEAD_PAD), BF16)
    return pl.pallas_call(
        _kv_expand_kernel, grid=(rows // tm,),
        in_specs=[c_spec, row(LANES), _const_spec(wk.shape), _const_spec(wv.shape),
                  _const_spec((1, HEAD_PAD))],
        out_specs=[head, head], out_shape=[hs, hs],
        compiler_params=_params(1), name="kv_expand")(c_stack, krpad, wk, wv, kg)


def _scores(q, k):
    batch_dims = tuple(range(q.ndim - 2))
    return lax.dot_general(q, k, (((q.ndim - 1,), (k.ndim - 1,)), (batch_dims, batch_dims)),
                           preferred_element_type=F32)


def _online_update(s, v, m, acc):
    m_new = jnp.maximum(m, jnp.max(s, axis=-1, keepdims=True))
    alpha = jnp.exp2(m - m_new)
    p = jnp.exp2(s - m_new).astype(BF16)
    batch_dims = tuple(range(p.ndim - 2))
    pv = lax.dot_general(p, v, (((p.ndim - 1,), (v.ndim - 2,)), (batch_dims, batch_dims)),
                         preferred_element_type=F32)
    return m_new, alpha * acc + pv


def _pair_output(acc_even, acc_odd):
    l_even = acc_even[:, _ones_lane(0):_ones_lane(0) + 1]
    l_odd = acc_odd[:, _ones_lane(1):_ones_lane(1) + 1]
    lane = lax.broadcasted_iota(jnp.int32, (1, HEAD_PAD), 1)
    return jnp.where(lane < V_DIM, acc_even / l_even, acc_odd / l_odd).astype(BF16)


HEADS_PER_STEP = 4
DIAG_STRIP = 256
SCORE_LOOKAHEAD = 2


def _prompt_attn_kernel(qt_ref, k_ref, vt_ref, o_ref, *, tq):
    seq = k_ref.shape[1]
    row = lax.broadcasted_iota(jnp.int32, (HEAD_PAD, 1), 0)
    heads = range(HEADS_PER_STEP)

    def update_all(ss, vts, carry):
        stats = []
        for s, (m, _) in zip(ss, carry):
            m_new = jnp.maximum(m, jnp.max(s, axis=0, keepdims=True))
            stats.append((m_new, jnp.exp2(m - m_new), jnp.exp2(s - m_new).astype(BF16)))
        return [(m_new, alpha * acc + jnp.dot(vt, p, preferred_element_type=F32))
                for (m_new, alpha, p), vt, (_, acc) in zip(stats, vts, carry)]

    n_strips = tq // DIAG_STRIP

    def scores(qi, step):
        q0 = qi * tq
        if step < qi:
            k_rows, q_cols = slice(step * tq, (step + 1) * tq), slice(q0, q0 + tq)
        else:
            c0 = (step - qi) * DIAG_STRIP
            k_rows, q_cols = slice(q0, q0 + c0 + DIAG_STRIP), slice(q0 + c0, q0 + c0 + DIAG_STRIP)
        return [jnp.dot(k_ref[hd, k_rows, :], qt_ref[hd, :, q_cols], preferred_element_type=F32)
                for hd in heads]

    schedule = [(qi, step) for qi in range(seq // tq) for step in range(qi + n_strips)]
    pending = [scores(*schedule[i]) for i in range(SCORE_LOOKAHEAD)]
    for t, (qi, step) in enumerate(schedule):
        q0 = qi * tq
        if step == 0:
            carry = [(jnp.full((1, tq), NEG_INF, F32), jnp.zeros((HEAD_PAD, tq), F32)) for _ in heads]
        ss = pending.pop(0)
        if t + SCORE_LOOKAHEAD < len(schedule):
            pending.append(scores(*schedule[t + SCORE_LOOKAHEAD]))
        if step < qi:
            carry = update_all(ss, [vt_ref[hd, step] for hd in heads], carry)
            continue
        c0 = (step - qi) * DIAG_STRIP
        n_keys = c0 + DIAG_STRIP
        key_chunk = lax.broadcasted_iota(jnp.int32, (n_keys, DIAG_STRIP), 0) // CHUNK
        qry_chunk = (lax.broadcasted_iota(jnp.int32, (n_keys, DIAG_STRIP), 1) + c0) // CHUNK
        mask = key_chunk <= qry_chunk
        strip = update_all([jnp.where(mask, s, NEG_INF) for s in ss],
                           [vt_ref[hd, qi, :, :n_keys] for hd in heads],
                           [(m[:, c0:c0 + DIAG_STRIP], acc[:, c0:c0 + DIAG_STRIP]) for m, acc in carry])
        for p in range(HEADS_PER_STEP // 2):
            acc_e, acc_o = strip[2 * p][1], strip[2 * p + 1][1]
            l_e = acc_e[_ones_lane(0):_ones_lane(0) + 1, :]
            l_o = acc_o[_ones_lane(1):_ones_lane(1) + 1, :]
            o_t = jnp.where(row < V_DIM, acc_e / l_e, acc_o / l_o)
            o_ref[q0 + c0:q0 + n_keys, p * LANES:(p + 1) * LANES] = o_t.T.astype(BF16)


def _prompt_attn_call(q_t, k, vt, batch, seq, tq=ATTN_BLOCK):
    tq = min(tq, seq)
    nq = seq // tq
    rows = batch * seq
    assert vt.shape == (N_HEADS, batch * nq, HEAD_PAD, tq)
    g = HEADS_PER_STEP
    return pl.pallas_call(
        functools.partial(_prompt_attn_kernel, tq=tq),
        grid=(batch, N_HEADS // g),
        in_specs=[pl.BlockSpec((g, HEAD_PAD, seq), lambda b, i: (i, 0, b)),
                  pl.BlockSpec((g, seq, HEAD_PAD), lambda b, i: (i, b, 0)),
                  pl.BlockSpec((g, nq, HEAD_PAD, tq), lambda b, i: (i, b, 0, 0))],
        out_specs=pl.BlockSpec((seq, g // 2 * LANES), lambda b, i: (b, i)),
        out_shape=jax.ShapeDtypeStruct((rows, D_MODEL), BF16),
        compiler_params=_params(2), name="prompt_attention")(q_t, k, vt)


def _expand_cache_block(c_ref, kr_ref, wkt_ref, wv_ref, kgt_ref, kt_dst, v_dst):
    place_t = (lax.broadcasted_iota(jnp.int32, (HEAD_PAD, ROPE_DIM), 0)
               == lax.broadcasted_iota(jnp.int32, (HEAD_PAD, ROPE_DIM), 1) + ROPE_LANE0).astype(BF16)
    contract_last = (((1,), (1,)), ((), ()))
    rope_rows = lax.dot_general(place_t, kr_ref[...].astype(BF16), contract_last,
                                preferred_element_type=F32)[NOPE_DIM:]
    c = c_ref[...].astype(BF16)
    k_t = lax.dot_general(wkt_ref[...], c, contract_last, preferred_element_type=F32)
    for hd in range(N_HEADS):
        blk = k_t[hd * NOPE_DIM:(hd + 1) * NOPE_DIM]
        r = lax.rsqrt(jnp.sum(blk * blk, axis=0, keepdims=True) * (1.0 / NOPE_DIM) + EPS)
        kt_dst[hd] = jnp.concatenate([blk * r * kgt_ref[...], rope_rows], axis=0).astype(BF16)
    _expand_values(c, wv_ref, v_dst)


def _sample_attn_kernel(q_ref, c0_ref, kr0_ref, c1_ref, kr1_ref, c2_ref, kr2_ref, wkt_ref, wv_ref, kgt_ref,
                        kn_ref, vn_ref, o_ref, kt_a, v_a, kt_b, v_b, m_scr, acc_scr):
    j = pl.program_id(1)
    last = pl.num_programs(1) - 1
    weights = (wkt_ref, wv_ref, kgt_ref)

    @pl.when(j == 0)
    def _():
        m_scr[...] = jnp.full(m_scr.shape, NEG_INF, F32)
        acc_scr[...] = jnp.zeros(acc_scr.shape, F32)
        _expand_cache_block(c0_ref, kr0_ref, *weights, kt_a, v_a)

    q = q_ref[...]

    def attend(kt_src, v_src, m, acc):
        s = lax.dot_general(q, kt_src[...], (((2,), (1,)), ((0,), (0,))), preferred_element_type=F32)
        return _online_update(s, v_src[...], m, acc)

    _expand_cache_block(c1_ref, kr1_ref, *weights, kt_b, v_b)
    m, acc = attend(kt_a, v_a, m_scr[...], acc_scr[...])
    _expand_cache_block(c2_ref, kr2_ref, *weights, kt_a, v_a)
    m, acc = attend(kt_b, v_b, m, acc)
    m_scr[...] = m
    acc_scr[...] = acc

    @pl.when(j == last)
    def _():
        _, acc_f = _online_update(_scores(q, kn_ref[...]), vn_ref[...], m, acc)
        for pair in range(N_HEADS // 2):
            o_ref[:, pair * LANES:(pair + 1) * LANES] = _pair_output(acc_f[2 * pair], acc_f[2 * pair + 1])


def _sample_attn_call(q, cache_c, cache_kr, layer, wk_t, wv, kg_t, kn, vn, batch, n_new, past, tk=512):
    tk = min(tk, past)
    nk = past // tk
    assert nk % 2 == 0
    new_spec = pl.BlockSpec((N_HEADS, n_new, HEAD_PAD), lambda b, j: (0, b, 0))

    def cache(w, block_of_step):
        return pl.BlockSpec((None, None, tk, w), lambda b, j: (layer, b, block_of_step(j), 0))

    first = lambda j: 0
    odd = lambda j: 2 * j + 1
    next_even = lambda j: jnp.minimum(2 * j + 2, nk - 1)
    cache_specs = [cache(w, f) for f in (first, odd, next_even) for w in (KV_LORA, ROPE_DIM)]
    kt_tile = pltpu.VMEM((N_HEADS, HEAD_PAD, tk), BF16)
    v_tile = pltpu.VMEM((N_HEADS, tk, HEAD_PAD), BF16)
    return pl.pallas_call(
        _sample_attn_kernel, grid=(batch, nk // 2),
        in_specs=[new_spec] + cache_specs + [_const_spec(wk_t.shape), _const_spec(wv.shape),
                                             _const_spec((NOPE_DIM, tk)), new_spec, new_spec],
        out_specs=pl.BlockSpec((n_new, D_MODEL), lambda b, j: (b, 0)),
        out_shape=jax.ShapeDtypeStruct((batch * n_new, D_MODEL), BF16),
        scratch_shapes=[kt_tile, v_tile, kt_tile, v_tile,
                        pltpu.VMEM((N_HEADS, n_new, 1), F32),
                        pltpu.VMEM((N_HEADS, n_new, HEAD_PAD), F32)],
        compiler_params=_params(2), name="sample_attention")(
            q, cache_c, cache_kr, cache_c, cache_kr, cache_c, cache_kr,
            wk_t, wv, jnp.broadcast_to(kg_t, (NOPE_DIM, tk)), kn, vn)


def _lane_vec(parts):
    v = jnp.concatenate(parts)
    return jnp.pad(v, (0, LANES - v.shape[0])).reshape(1, LANES)


def _rope_tables(pos):
    inv = ROPE_BASE ** (-jnp.arange(HALF_ROPE, dtype=F32) / HALF_ROPE)
    ang = pos.astype(F32)[:, None] * inv[None, :]
    cos, sin = jnp.cos(ang), jnp.sin(ang)
    n = pos.shape[0]
    zeros = lambda w: jnp.zeros((n, w), F32)
    cos_t = jnp.concatenate([jnp.ones((n, NOPE_DIM), F32), cos, cos, zeros(ROPE_TAIL)], axis=1)
    sin_t = jnp.concatenate([zeros(ROPE_LANE0), -sin, sin, zeros(ROPE_TAIL)], axis=1)
    return cos_t, sin_t, cos.T, sin.T


def _prep_mla(w_in, w_uq, w_ukv, q_nope_g, q_rope_g, k_nope_g, k_rope_g):
    zeros = lambda n: jnp.zeros((n,), F32)
    swap = lambda g: jnp.concatenate([g[HALF_ROPE:], g[:HALF_ROPE]])
    w_q, w_kv = w_in[:, :Q_LORA], w_in[:, Q_LORA:Q_LORA + KV_LORA]
    w_kr1 = w_in[:, Q_LORA + KV_LORA:Q_LORA + KV_LORA + HALF_ROPE]
    w_kr2 = w_in[:, Q_LORA + KV_LORA + HALF_ROPE:]
    zc = lambda n: jnp.zeros((D_MODEL, n), F32)
    win = jnp.concatenate([w_q, w_kv, zc(ROPE_LANE0), w_kr1, w_kr2, zc(ROPE_TAIL),
                           zc(ROPE_LANE0), w_kr2, w_kr1, zc(ROPE_TAIL)], axis=1).astype(BF16)
    gains = dict(krg=_lane_vec([zeros(ROPE_LANE0), k_rope_g]),
                 krgs=_lane_vec([zeros(ROPE_LANE0), swap(k_rope_g)]),
                 qhg=_lane_vec([q_nope_g, q_rope_g]) * (ATTN_SCALE * LOG2E),
                 qhgs=_lane_vec([zeros(ROPE_LANE0), swap(q_rope_g)]) * (ATTN_SCALE * LOG2E))

    uq = w_uq.reshape(Q_LORA, N_HEADS, QK_DIM)
    uq_n, uq_1, uq_2 = uq[..., :NOPE_DIM], uq[..., NOPE_DIM:NOPE_DIM + HALF_ROPE], uq[..., NOPE_DIM + HALF_ROPE:]
    zq = lambda n: jnp.zeros((Q_LORA, N_HEADS, n), F32)
    main = jnp.concatenate([uq_n, uq_1, uq_2, zq(ROPE_TAIL)], axis=-1).reshape(Q_LORA, Q_PAD)
    swapped = jnp.concatenate([zq(ROPE_LANE0), uq_2, uq_1, zq(ROPE_TAIL)], axis=-1).reshape(Q_LORA, Q_PAD)
    wuq = jnp.concatenate([main, swapped], axis=1).astype(BF16)
    wuq_t = main.T.astype(BF16)

    ukv = w_ukv.reshape(KV_LORA, N_HEADS // 2, 2, NOPE_DIM + V_DIM)
    zk = jnp.zeros((KV_LORA, N_HEADS // 2, 2, HEAD_PAD - NOPE_DIM), F32)
    wk = jnp.concatenate([ukv[..., :NOPE_DIM], zk], axis=-1).reshape(KV_LORA, Q_PAD).astype(BF16)
    wk_t = ukv[..., :NOPE_DIM].reshape(KV_LORA, N_HEADS * NOPE_DIM).T.astype(BF16)
    zv = jnp.zeros((KV_LORA, N_HEADS // 2, V_DIM), F32)
    v_even = jnp.concatenate([ukv[:, :, 0, NOPE_DIM:], zv], axis=-1)
    v_odd = jnp.concatenate([zv, ukv[:, :, 1, NOPE_DIM:]], axis=-1)
    wv = jnp.stack([v_even, v_odd], axis=2).reshape(KV_LORA, Q_PAD).astype(BF16)
    gains.update(q_nope=q_nope_g * (ATTN_SCALE * LOG2E), q_rope=q_rope_g * (ATTN_SCALE * LOG2E))
    gains.update(k_nope_col=k_nope_g[:, None])
    return win, wuq, wuq_t, wk, wk_t, wv, _lane_vec([k_nope_g]), gains


def _key_tables(cos_t, sin_t, gains):
    return cos_t * gains["krg"], sin_t * gains["krgs"]


def _query_tables(cos_t, sin_t, gains):
    return cos_t * gains["qhg"], sin_t * gains["qhgs"]


def _query_tables_t(cos_tt, sin_tt, gains, tm):
    g1, g2 = gains["q_rope"][:HALF_ROPE, None], gains["q_rope"][HALF_ROPE:, None]
    rot = jnp.concatenate([g1 * cos_tt, g2 * sin_tt, g2 * cos_tt, g1 * sin_tt], axis=0)
    return jnp.broadcast_to(gains["q_nope"][:, None], (NOPE_DIM, tm)), rot


def kernel(x_prompt, x_sample, cache_kv_latent, cache_k_rope, mix_norm_g, mlp_norm_g, sgu_w_in, sgu_norm_g, sgu_w_s, sgu_b_s, sgu_w_out, mla_w_in, mla_q_norm_g, mla_kv_norm_g, mla_w_uq, mla_w_ukv, mla_q_nope_g, mla_q_rope_g, mla_k_nope_g, mla_k_rope_g, mla_w_o, mlp_w_up, mlp_w_down):
    batch, seq, _ = x_prompt.shape
    dec_batch, n_new, _ = x_sample.shape
    past = cache_kv_latent.shape[2]
    depth = mix_norm_g.shape[0]
    rows_p, rows_s = batch * seq, dec_batch * n_new

    xp = x_prompt.reshape(rows_p, D_MODEL)
    xs = x_sample.reshape(rows_s, D_MODEL)
    cos_p, sin_p, cos_pt, sin_pt = _rope_tables(jnp.arange(seq))
    cos_s, sin_s, _, _ = _rope_tables(past + jnp.arange(n_new))
    cos_s, sin_s = jnp.tile(cos_s, (dec_batch, 1)), jnp.tile(sin_s, (dec_batch, 1))
    mlp_wu, mlp_wd = mlp_w_up.astype(BF16), mlp_w_down.astype(BF16)
    sgu_win, sgu_wout = sgu_w_in.astype(BF16), sgu_w_out.astype(BF16)
    mla_wo = mla_w_o.astype(BF16)
    reps = SGU_CHUNK // n_new
    sgu_ws_s = jnp.tile(sgu_w_s[:, :, :n_new, :n_new], (1, 1, reps, reps))
    sgu_b_p = sgu_b_s[..., None]
    sgu_b_smp = jnp.tile(sgu_b_s[:, :, :n_new], (1, 1, reps))[..., None]

    n_mla = mla_w_in.shape[0]
    cache_p = cache_s = None
    sgu_v_s = []
    for i in range(depth):
        j = i // 2
        g_mix = mix_norm_g[i].reshape(1, D_MODEL)
        g_mlp = mlp_norm_g[i].reshape(1, D_MODEL)
        if i % 2 == 0:
            ng = sgu_norm_g[j].reshape(1, D_SGU)
            xp = _sgu_call(xp, g_mix, sgu_win, ng, sgu_w_s, sgu_b_p, sgu_wout, j,
                           block_diag=False, emit_v=False)
            xs, v_new = _sgu_call(xs, g_mix, sgu_win, ng, sgu_ws_s, sgu_b_smp, sgu_wout, j,
                                  block_diag=True, emit_v=True)
            sgu_v_s.append(v_new.reshape(dec_batch, n_new, D_SGU))
            xp = _mlp_call(xp, g_mlp, mlp_wu, mlp_wd, i)
            xs = _mlp_call(xs, g_mlp, mlp_wu, mlp_wd, i)
        else:
            win, wuq, wuq_t, wk, wk_t, wv, kg, gains = _prep_mla(
                mla_w_in[j], mla_w_uq[j], mla_w_ukv[j], mla_q_nope_g[j], mla_q_rope_g[j],
                mla_k_nope_g[j], mla_k_rope_g[j])
            qg, kvg = mla_q_norm_g[j].reshape(1, -1), mla_kv_norm_g[j].reshape(1, -1)
            *cache_p, q_t, k, vt = _mla_proj_call(
                xp, g_mix, win, qg, kvg, _key_tables(cos_p, sin_p, gains), wuq_t,
                _query_tables_t(cos_pt, sin_pt, gains, min(ATTN_BLOCK, seq)), j, n_mla, cache_p,
                kv_weights=(wk, wv.T, kg))
            attn = _prompt_attn_call(q_t, k, vt, batch, seq)
            xp = _mlp_call(xp, g_mlp, mlp_wu, mlp_wd, i, attn=attn, wo=mla_wo, wo_layer=j)
            *cache_s, krpad, q = _mla_proj_call(
                xs, g_mix, win, qg, kvg, _key_tables(cos_s, sin_s, gains), wuq,
                _query_tables(cos_s, sin_s, gains), j, n_mla, cache_s)
            kn, vn = _kv_expand_call(cache_s[0], j, krpad, wk, wv, kg)
            attn = _sample_attn_call(q, cache_kv_latent, cache_k_rope, j, wk_t, wv, gains["k_nope_col"],
                                     kn, vn, dec_batch, n_new, past)
            xs = _mlp_call(xs, g_mlp, mlp_wu, mlp_wd, i, attn=attn, wo=mla_wo, wo_layer=j)
    return (xp.reshape(batch, seq, D_MODEL), xs.reshape(dec_batch, n_new, D_MODEL),
            cache_p[0].reshape(n_mla, batch, seq, KV_LORA), cache_p[1].reshape(n_mla, batch, seq, ROPE_DIM),
            cache_s[0].reshape(n_mla, dec_batch, n_new, KV_LORA),
            cache_s[1].reshape(n_mla, dec_batch, n_new, ROPE_DIM), jnp.stack(sgu_v_s))
```

```python
import functools
import math

import jax
import jax.numpy as jnp
import numpy as np
from jax import lax
from jax.experimental import pallas as pl
from jax.experimental.pallas import tpu as pltpu

F32 = jnp.float32
BF16 = jnp.bfloat16

D_MODEL = 1024
CHUNK = 64
SGU_CHUNK = 128
D_SGU = 2 * D_MODEL
SGU_GROUP_DIM = 128
SGU_GROUPS = D_SGU // SGU_GROUP_DIM
N_HEADS = 16
Q_LORA = 384
KV_LORA = 256
NOPE_DIM = 64
ROPE_DIM = 32
HALF_ROPE = ROPE_DIM // 2
V_DIM = 64
QK_DIM = NOPE_DIM + ROPE_DIM
ATTN_SCALE = 1.0 / math.sqrt(QK_DIM)
ROPE_BASE = 10000.0
D_FF = 4 * D_MODEL
EPS = 1e-6
NEG_INF = -1e30

LANES = 128
HEAD_PAD = LANES
FF_CHUNK = 1024
SGU_COL_CHUNK = 512
VMEM_LIMIT = 56 * 1024 * 1024
ATTN_BLOCK = 512


def _params(n_axes):
    return pltpu.CompilerParams(
        dimension_semantics=("arbitrary",) * n_axes, vmem_limit_bytes=VMEM_LIMIT)


def _const_spec(shape):
    nd = len(shape)
    return pl.BlockSpec(shape, lambda *_: (0,) * nd, pipeline_mode=pl.Buffered(1))


def _layer_spec(stack, layer):
    shape = stack.shape[1:]
    return pl.BlockSpec((None,) + shape, lambda *_: (layer,) + (0,) * len(shape),
                        pipeline_mode=pl.Buffered(1))


def _row_tile(rows, want):
    tm = min(rows, want)
    assert rows % tm == 0
    return tm


def _rms(xf, g, n):
    ss = jnp.sum(xf * xf, axis=-1, keepdims=True)
    return xf * lax.rsqrt(ss * (1.0 / n) + EPS) * g


def _gelu(z):
    return 0.5 * z * (1.0 + lax.erf(z * np.float32(math.sqrt(0.5))))


def _mlp_delta(x, g_ref, wu_ref, wd_ref):
    h = _rms(x, g_ref[...], D_MODEL).astype(BF16)
    acc = None
    for f in range(D_FF // FF_CHUNK):
        sl = slice(f * FF_CHUNK, (f + 1) * FF_CHUNK)
        a = jnp.dot(h, wu_ref[:, sl], preferred_element_type=F32)
        a = jnp.square(jnp.maximum(a, 0.0)).astype(BF16)
        d = jnp.dot(a, wd_ref[sl, :], preferred_element_type=F32)
        acc = d if acc is None else acc + d
    return acc


def _mlp_kernel(x_ref, g_ref, wu_ref, wd_ref, o_ref):
    x = x_ref[...]
    o_ref[...] = x + _mlp_delta(x, g_ref, wu_ref, wd_ref)


def _attn_out_mlp_kernel(x_ref, a_ref, wo_ref, g_ref, wu_ref, wd_ref, o_ref):
    x = x_ref[...] + jnp.dot(a_ref[...], wo_ref[...], preferred_element_type=F32)
    o_ref[...] = x + _mlp_delta(x, g_ref, wu_ref, wd_ref)


def _mlp_call(x, g, wu, wd, layer, attn=None, wo=None, wo_layer=None, tm_want=512):
    rows = x.shape[0]
    tm = _row_tile(rows, tm_want)
    row_spec = pl.BlockSpec((tm, D_MODEL), lambda i: (i, 0))
    w_specs = [_const_spec((1, D_MODEL)), _layer_spec(wu, layer), _layer_spec(wd, layer)]
    if attn is None:
        kern, ins, specs = _mlp_kernel, (x, g, wu, wd), [row_spec] + w_specs
        name = "channel_mlp"
    else:
        kern, ins = _attn_out_mlp_kernel, (x, attn, wo, g, wu, wd)
        specs = [row_spec, row_spec, _layer_spec(wo, wo_layer)] + w_specs
        name = "attn_out_channel_mlp"
    return pl.pallas_call(
        kern, grid=(rows // tm,), in_specs=specs, out_specs=row_spec,
        out_shape=jax.ShapeDtypeStruct((rows, D_MODEL), F32),
        compiler_params=_params(1), name=name)(*ins)


def _sgu_kernel(x_ref, g_ref, win_ref, ng_ref, ws_ref, b_ref, wout_ref, *rest, block_diag, emit_v):
    if emit_v:
        o_ref, v_ref, u_scr, v_scr, y_scr = rest
    else:
        o_ref, u_scr, v_scr, y_scr = rest
        v_ref = None
    tm = x_ref.shape[0]
    n_chunks = tm // SGU_CHUNK
    x = x_ref[...]
    h = _rms(x, g_ref[...], D_MODEL).astype(BF16)

    ss = jnp.zeros((tm, 1), F32)
    for k in range(D_SGU // SGU_COL_CHUNK):
        sl = slice(k * SGU_COL_CHUNK, (k + 1) * SGU_COL_CHUNK)
        u_scr[:, sl] = _gelu(jnp.dot(h, win_ref[:, sl], preferred_element_type=F32))
        slv = slice(D_SGU + k * SGU_COL_CHUNK, D_SGU + (k + 1) * SGU_COL_CHUNK)
        zv = _gelu(jnp.dot(h, win_ref[:, slv], preferred_element_type=F32))
        ss = ss + jnp.sum(zv * zv, axis=-1, keepdims=True)
        v_scr[:, sl] = zv
    r = lax.rsqrt(ss * (1.0 / D_SGU) + EPS)

    ri = lax.broadcasted_iota(jnp.int32, (SGU_CHUNK, SGU_CHUNK), 0) // CHUNK
    ci = lax.broadcasted_iota(jnp.int32, (SGU_CHUNK, SGU_CHUNK), 1) // CHUNK
    mask = (ri == ci) if block_diag else (ri >= ci)
    for grp in range(SGU_GROUPS):
        gsl = slice(grp * SGU_GROUP_DIM, (grp + 1) * SGU_GROUP_DIM)
        ws = jnp.where(mask, ws_ref[grp], 0.0).astype(BF16)
        pieces = []
        for c in range(n_chunks):
            rsl = slice(c * SGU_CHUNK, (c + 1) * SGU_CHUNK)
            vn = v_scr[rsl, gsl] * r[rsl] * ng_ref[:, gsl]
            if emit_v:
                v_ref[rsl, gsl] = vn
            pieces.append(vn.astype(BF16))
        rhs = pieces[0] if n_chunks == 1 else jnp.concatenate(pieces, axis=1)
        mixed = jnp.dot(ws, rhs, preferred_element_type=F32) + b_ref[grp]
        for c in range(n_chunks):
            rsl = slice(c * SGU_CHUNK, (c + 1) * SGU_CHUNK)
            y = u_scr[rsl, gsl] * mixed[:, c * SGU_CHUNK:(c + 1) * SGU_CHUNK]
            y_scr[rsl, gsl] = y.astype(BF16)
    o_ref[...] = x + jnp.dot(y_scr[...], wout_ref[...], preferred_element_type=F32)


def _sgu_call(x, g, win, ng, ws, b, wout, layer, *, block_diag, emit_v, tm_want=512):
    rows = x.shape[0]
    tm = _row_tile(rows, tm_want)
    row_spec = pl.BlockSpec((tm, D_MODEL), lambda i: (i, 0))
    specs = [row_spec, _const_spec((1, D_MODEL)), _layer_spec(win, layer), _const_spec((1, D_SGU)),
             _layer_spec(ws, layer), _layer_spec(b, layer), _layer_spec(wout, layer)]
    out_shape = [jax.ShapeDtypeStruct((rows, D_MODEL), F32)]
    out_specs = [row_spec]
    if emit_v:
        out_shape.append(jax.ShapeDtypeStruct((rows, D_SGU), F32))
        out_specs.append(pl.BlockSpec((tm, D_SGU), lambda i: (i, 0)))
    res = pl.pallas_call(
        functools.partial(_sgu_kernel, block_diag=block_diag, emit_v=emit_v),
        grid=(rows // tm,), in_specs=specs, out_specs=out_specs, out_shape=out_shape,
        scratch_shapes=[pltpu.VMEM((tm, D_SGU), F32), pltpu.VMEM((tm, D_SGU), F32),
                        pltpu.VMEM((tm, D_SGU), BF16)],
        compiler_params=_params(1), name="sgu_mixer_v" if emit_v else "sgu_mixer")(
            x, g, win, ng, ws, b, wout)
    return res if emit_v else res[0]


ROPE_LANE0 = NOPE_DIM
ROPE_TAIL = HEAD_PAD - ROPE_LANE0 - ROPE_DIM
W_IN_PAD = Q_LORA + KV_LORA + 2 * LANES
Q_PAD = N_HEADS * HEAD_PAD
LOG2E = math.log2(math.e)


def _ones_lane(hd):
    return V_DIM if hd % 2 == 0 else 0


def _mla_proj_kernel(x_ref, g_ref, win_ref, qg_ref, kvg_ref, kcos_ref, ksin_ref, wuq_ref,
                     qtab_a_ref, qtab_b_ref, *rest, prompt_layout):
    if prompt_layout:
        wk_ref, wvt_ref, kg_ref = rest[:3]
        ckv_ref, kr_ref, q_ref, k_ref, vt_ref = rest[-5:]
    else:
        ckv_ref, kr_ref, krpad_ref, q_ref = rest[-4:]
    x = x_ref[...]
    h = _rms(x, g_ref[...], D_MODEL).astype(BF16)
    a = jnp.dot(h, win_ref[...], preferred_element_type=F32)
    c_q = _rms(a[:, :Q_LORA], qg_ref[...], Q_LORA).astype(BF16)
    c_kv = _rms(a[:, Q_LORA:Q_LORA + KV_LORA], kvg_ref[...], KV_LORA)
    ckv_ref[...] = c_kv

    kr = a[:, Q_LORA + KV_LORA:Q_LORA + KV_LORA + LANES]
    kr_sw = a[:, Q_LORA + KV_LORA + LANES:]
    r_kr = lax.rsqrt(jnp.sum(kr * kr, axis=-1, keepdims=True) * (1.0 / ROPE_DIM) + EPS)
    kr_rot = (kr * r_kr) * kcos_ref[...] + (kr_sw * r_kr) * ksin_ref[...]
    kr_ref[...] = kr_rot[:, ROPE_LANE0:ROPE_LANE0 + ROPE_DIM]

    if prompt_layout:
        c = c_kv.astype(BF16)
        _expand_keys(c, kr_rot, wk_ref, kg_ref, k_ref)
        _expand_values_t(c, wvt_ref, vt_ref)
        q_t = lax.dot_general(wuq_ref[...], c_q, (((1,), (1,)), ((), ())), preferred_element_type=F32)
        g_nope = qtab_a_ref[...]
        rot = qtab_b_ref[...]
        t_a, t_b, t_c, t_d = (rot[i * HALF_ROPE:(i + 1) * HALF_ROPE] for i in range(4))
        pad = jnp.zeros((ROPE_TAIL, q_t.shape[1]), F32)
        for hd in range(N_HEADS):
            base = hd * HEAD_PAD
            nope = q_t[base:base + NOPE_DIM]
            x1 = q_t[base + NOPE_DIM:base + NOPE_DIM + HALF_ROPE]
            x2 = q_t[base + NOPE_DIM + HALF_ROPE:base + QK_DIM]
            s_n = jnp.sum(nope * nope, axis=0, keepdims=True)
            s_r = jnp.sum(x1 * x1 + x2 * x2, axis=0, keepdims=True)
            r_n = lax.rsqrt(s_n * (1.0 / NOPE_DIM) + EPS)
            r_r = lax.rsqrt(s_r * (1.0 / ROPE_DIM) + EPS)
            x1, x2 = x1 * r_r, x2 * r_r
            q_ref[hd] = jnp.concatenate(
                [nope * r_n * g_nope, x1 * t_a - x2 * t_b, x2 * t_c + x1 * t_d, pad], axis=0).astype(BF16)
        return

    krpad_ref[...] = kr_rot.astype(BF16)
    qq = jnp.dot(c_q, wuq_ref[...], preferred_element_type=F32)
    qcos = qtab_a_ref[...]
    qsin = qtab_b_ref[...]
    is_nope = lax.broadcasted_iota(jnp.int32, (1, HEAD_PAD), 1) < NOPE_DIM
    for hd in range(N_HEADS):
        blk = qq[:, hd * HEAD_PAD:(hd + 1) * HEAD_PAD]
        blk_sw = qq[:, Q_PAD + hd * HEAD_PAD:Q_PAD + (hd + 1) * HEAD_PAD]
        sq = blk * blk
        s_n = jnp.sum(jnp.where(is_nope, sq, 0.0), axis=-1, keepdims=True)
        s_r = jnp.sum(jnp.where(is_nope, 0.0, sq), axis=-1, keepdims=True)
        r_n = lax.rsqrt(s_n * (1.0 / NOPE_DIM) + EPS)
        r_r = lax.rsqrt(s_r * (1.0 / ROPE_DIM) + EPS)
        q_ref[hd] = ((blk * jnp.where(is_nope, r_n, r_r)) * qcos + (blk_sw * r_r) * qsin).astype(BF16)


def _mla_proj_call(x, g, win, qg, kvg, k_tables, wuq, q_tables, layer, n_layers, prev=None, *,
                   kv_weights=None, tm_want=ATTN_BLOCK):
    prompt_layout = kv_weights is not None
    rows = x.shape[0]
    period = k_tables[0].shape[0]
    tm = _row_tile(min(rows, period), tm_want)
    n_per = period // tm
    row = lambda w: pl.BlockSpec((tm, w), lambda i: (i, 0))
    slot = lambda w: pl.BlockSpec((None, tm, w), lambda i: (layer, i, 0))
    tab = pl.BlockSpec((tm, HEAD_PAD), lambda i: (i % n_per, 0))
    head = pl.BlockSpec((N_HEADS, tm, HEAD_PAD), lambda i: (0, i, 0))
    head_shape = jax.ShapeDtypeStruct((N_HEADS, rows, HEAD_PAD), BF16)
    specs = [row(D_MODEL), _const_spec((1, D_MODEL)), _const_spec((D_MODEL, W_IN_PAD)),
             _const_spec((1, Q_LORA)), _const_spec((1, KV_LORA)), tab, tab, _const_spec(wuq.shape)]
    ins = [x, g, win, qg, kvg, *k_tables, wuq, *q_tables]
    out_shape = [jax.ShapeDtypeStruct((n_layers, rows, KV_LORA), F32),
                 jax.ShapeDtypeStruct((n_layers, rows, ROPE_DIM), F32)]
    out_specs = [slot(KV_LORA), slot(ROPE_DIM)]
    if prompt_layout:
        specs += [_const_spec((NOPE_DIM, tm)), pl.BlockSpec((2 * ROPE_DIM, tm), lambda i: (0, i % n_per))]
        specs += [_const_spec(w.shape) for w in kv_weights]
        ins += list(kv_weights)
        out_shape += [jax.ShapeDtypeStruct((N_HEADS, HEAD_PAD, rows), BF16), head_shape,
                      jax.ShapeDtypeStruct((N_HEADS, rows // tm, HEAD_PAD, tm), BF16)]
        out_specs += [pl.BlockSpec((N_HEADS, HEAD_PAD, tm), lambda i: (0, 0, i)), head,
                      pl.BlockSpec((N_HEADS, 1, HEAD_PAD, tm), lambda i: (0, i, 0, 0))]
    else:
        specs += [tab, tab]
        out_shape += [jax.ShapeDtypeStruct((rows, LANES), BF16), head_shape]
        out_specs += [row(LANES), head]
    aliases = {}
    if prev is not None:
        aliases = {len(ins): 0, len(ins) + 1: 1}
        ins += list(prev)
        specs += [pl.BlockSpec(memory_space=pl.ANY)] * 2
    return pl.pallas_call(
        functools.partial(_mla_proj_kernel, prompt_layout=prompt_layout), grid=(rows // tm,),
        in_specs=specs, out_specs=out_specs, out_shape=out_shape, input_output_aliases=aliases,
        compiler_params=_params(1), name="mla_project_kv" if prompt_layout else "mla_project")(*ins)


def _expand_keys(c, krp, wk_ref, kg_ref, k_dst):
    kk = jnp.dot(c, wk_ref[...], preferred_element_type=F32)
    for hd in range(N_HEADS):
        blk = kk[:, hd * HEAD_PAD:(hd + 1) * HEAD_PAD]
        r = lax.rsqrt(jnp.sum(blk * blk, axis=-1, keepdims=True) * (1.0 / NOPE_DIM) + EPS)
        k_dst[hd] = (blk * r * kg_ref[...] + krp).astype(BF16)


def _expand_values(c, wv_ref, v_dst):
    vv = jnp.dot(c, wv_ref[...], preferred_element_type=F32)
    lane = lax.broadcasted_iota(jnp.int32, (1, HEAD_PAD), 1)
    for hd in range(N_HEADS):
        vblk = vv[:, hd * HEAD_PAD:(hd + 1) * HEAD_PAD]
        v_dst[hd] = jnp.where(lane == _ones_lane(hd), 1.0, vblk).astype(BF16)


def _expand_values_t(c, wvt_ref, vt_dst):
    vt = lax.dot_general(wvt_ref[...], c, (((1,), (1,)), ((), ())), preferred_element_type=F32)
    row = lax.broadcasted_iota(jnp.int32, (HEAD_PAD, 1), 0)
    for hd in range(N_HEADS):
        blk = vt[hd * HEAD_PAD:(hd + 1) * HEAD_PAD, :]
        vt_dst[hd, 0] = jnp.where(row == _ones_lane(hd), 1.0, blk).astype(BF16)


def _kv_expand_kernel(c_ref, krpad_ref, wk_ref, wv_ref, kg_ref, k_ref, v_ref):
    c = c_ref[...].astype(BF16)
    _expand_keys(c, krpad_ref[...].astype(F32), wk_ref, kg_ref, k_ref)
    _expand_values(c, wv_ref, v_ref)


def _kv_expand_call(c_stack, layer, krpad, wk, wv, kg, tm_want=512):
    rows = c_stack.shape[1]
    tm = _row_tile(rows, tm_want)
    row = lambda w: pl.BlockSpec((tm, w), lambda i: (i, 0))
    c_spec = pl.BlockSpec((None, tm, KV_LORA), lambda i: (layer, i, 0))
    head = pl.BlockSpec((N_HEADS, tm, HEAD_PAD), lambda i: (0, i, 0))
    hs = jax.ShapeDtypeStruct((N_HEADS, rows, HEAD_PAD), BF16)
    return pl.pallas_call(
        _kv_expand_kernel, grid=(rows // tm,),
        in_specs=[c_spec, row(LANES), _const_spec(wk.shape), _const_spec(wv.shape),
                  _const_spec((1, HEAD_PAD))],
        out_specs=[head, head], out_shape=[hs, hs],
        compiler_params=_params(1), name="kv_expand")(c_stack, krpad, wk, wv, kg)


def _scores(q, k):
    batch_dims = tuple(range(q.ndim - 2))
    return lax.dot_general(q, k, (((q.ndim - 1,), (k.ndim - 1,)), (batch_dims, batch_dims)),
                           preferred_element_type=F32)


def _online_update(s, v, m, acc):
    m_new = jnp.maximum(m, jnp.max(s, axis=-1, keepdims=True))
    alpha = jnp.exp2(m - m_new)
    p = jnp.exp2(s - m_new).astype(BF16)
    batch_dims = tuple(range(p.ndim - 2))
    pv = lax.dot_general(p, v, (((p.ndim - 1,), (v.ndim - 2,)), (batch_dims, batch_dims)),
                         preferred_element_type=F32)
    return m_new, alpha * acc + pv


def _pair_output(acc_even, acc_odd):
    l_even = acc_even[:, _ones_lane(0):_ones_lane(0) + 1]
    l_odd = acc_odd[:, _ones_lane(1):_ones_lane(1) + 1]
    lane = lax.broadcasted_iota(jnp.int32, (1, HEAD_PAD), 1)
    return jnp.where(lane < V_DIM, acc_even / l_even, acc_odd / l_odd).astype(BF16)


HEADS_PER_STEP = 4
DIAG_STRIP = 256
SCORE_LOOKAHEAD = 2


def _prompt_attn_kernel(qt_ref, k_ref, vt_ref, o_ref, *, tq):
    seq = k_ref.shape[1]
    row = lax.broadcasted_iota(jnp.int32, (HEAD_PAD, 1), 0)
    heads = range(HEADS_PER_STEP)

    def update_all(ss, vts, carry):
        stats = []
        for s, (m, _) in zip(ss, carry):
            m_new = jnp.maximum(m, jnp.max(s, axis=0, keepdims=True))
            stats.append((m_new, jnp.exp2(m - m_new), jnp.exp2(s - m_new).astype(BF16)))
        return [(m_new, alpha * acc + jnp.dot(vt, p, preferred_element_type=F32))
                for (m_new, alpha, p), vt, (_, acc) in zip(stats, vts, carry)]

    n_strips = tq // DIAG_STRIP

    def scores(qi, step):
        q0 = qi * tq
        if step < qi:
            k_rows, q_cols = slice(step * tq, (step + 1) * tq), slice(q0, q0 + tq)
        else:
            c0 = (step - qi) * DIAG_STRIP
            k_rows, q_cols = slice(q0, q0 + c0 + DIAG_STRIP), slice(q0 + c0, q0 + c0 + DIAG_STRIP)
        return [jnp.dot(k_ref[hd, k_rows, :], qt_ref[hd, :, q_cols], preferred_element_type=F32)
                for hd in heads]

    schedule = [(qi, step) for qi in range(seq // tq) for step in range(qi + n_strips)]
    pending = [scores(*schedule[i]) for i in range(SCORE_LOOKAHEAD)]
    for t, (qi, step) in enumerate(schedule):
        q0 = qi * tq
        if step == 0:
            carry = [(jnp.full((1, tq), NEG_INF, F32), jnp.zeros((HEAD_PAD, tq), F32)) for _ in heads]
        ss = pending.pop(0)
        if t + SCORE_LOOKAHEAD < len(schedule):
            pending.append(scores(*schedule[t + SCORE_LOOKAHEAD]))
        if step < qi:
            carry = update_all(ss, [vt_ref[hd, step] for hd in heads], carry)
            continue
        c0 = (step - qi) * DIAG_STRIP
        n_keys = c0 + DIAG_STRIP
        key_chunk = lax.broadcasted_iota(jnp.int32, (n_keys, DIAG_STRIP), 0) // CHUNK
        qry_chunk = (lax.broadcasted_iota(jnp.int32, (n_keys, DIAG_STRIP), 1) + c0) // CHUNK
        mask = key_chunk <= qry_chunk
        strip = update_all([jnp.where(mask, s, NEG_INF) for s in ss],
                           [vt_ref[hd, qi, :, :n_keys] for hd in heads],
                           [(m[:, c0:c0 + DIAG_STRIP], acc[:, c0:c0 + DIAG_STRIP]) for m, acc in carry])
        for p in range(HEADS_PER_STEP // 2):
            acc_e, acc_o = strip[2 * p][1], strip[2 * p + 1][1]
            l_e = acc_e[_ones_lane(0):_ones_lane(0) + 1, :]
            l_o = acc_o[_ones_lane(1):_ones_lane(1) + 1, :]
            o_t = jnp.where(row < V_DIM, acc_e / l_e, acc_o / l_o)
            o_ref[q0 + c0:q0 + n_keys, p * LANES:(p + 1) * LANES] = o_t.T.astype(BF16)


def _prompt_attn_call(q_t, k, vt, batch, seq, tq=ATTN_BLOCK):
    tq = min(tq, seq)
    nq = seq // tq
    rows = batch * seq
    assert vt.shape == (N_HEADS, batch * nq, HEAD_PAD, tq)
    g = HEADS_PER_STEP
    return pl.pallas_call(
        functools.partial(_prompt_attn_kernel, tq=tq),
        grid=(batch, N_HEADS // g),
        in_specs=[pl.BlockSpec((g, HEAD_PAD, seq), lambda b, i: (i, 0, b)),
                  pl.BlockSpec((g, seq, HEAD_PAD), lambda b, i: (i, b, 0)),
                  pl.BlockSpec((g, nq, HEAD_PAD, tq), lambda b, i: (i, b, 0, 0))],
        out_specs=pl.BlockSpec((seq, g // 2 * LANES), lambda b, i: (b, i)),
        out_shape=jax.ShapeDtypeStruct((rows, D_MODEL), BF16),
        compiler_params=_params(2), name="prompt_attention")(q_t, k, vt)


def _expand_cache_block(c_ref, kr_ref, wkt_ref, wv_ref, kgt_ref, kt_dst, v_dst):
    place_t = (lax.broadcasted_iota(jnp.int32, (HEAD_PAD, ROPE_DIM), 0)
               == lax.broadcasted_iota(jnp.int32, (HEAD_PAD, ROPE_DIM), 1) + ROPE_LANE0).astype(BF16)
    contract_last = (((1,), (1,)), ((), ()))
    rope_rows = lax.dot_general(place_t, kr_ref[...].astype(BF16), contract_last,
                                preferred_element_type=F32)[NOPE_DIM:]
    c = c_ref[...].astype(BF16)
    k_t = lax.dot_general(wkt_ref[...], c, contract_last, preferred_element_type=F32)
    for hd in range(N_HEADS):
        blk = k_t[hd * NOPE_DIM:(hd + 1) * NOPE_DIM]
        r = lax.rsqrt(jnp.sum(blk * blk, axis=0, keepdims=True) * (1.0 / NOPE_DIM) + EPS)
        kt_dst[hd] = jnp.concatenate([blk * r * kgt_ref[...], rope_rows], axis=0).astype(BF16)
    _expand_values(c, wv_ref, v_dst)


def _sample_attn_kernel(q_ref, c_ref, kr_ref, wkt_ref, wv_ref, kgt_ref, kn_ref, vn_ref, o_ref,
                        kt_scr, v_scr, m_scr, acc_scr):
    kb = pl.program_id(1)
    last = pl.num_programs(1) - 1

    @pl.when(kb == 0)
    def _():
        m_scr[...] = jnp.full(m_scr.shape, NEG_INF, F32)
        acc_scr[...] = jnp.zeros(acc_scr.shape, F32)

    _expand_cache_block(c_ref, kr_ref, wkt_ref, wv_ref, kgt_ref, kt_scr, v_scr)
    q = q_ref[...]
    s = lax.dot_general(q, kt_scr[...], (((2,), (1,)), ((0,), (0,))), preferred_element_type=F32)
    m, acc = _online_update(s, v_scr[...], m_scr[...], acc_scr[...])
    m_scr[...] = m
    acc_scr[...] = acc

    @pl.when(kb == last)
    def _():
        _, acc_f = _online_update(_scores(q, kn_ref[...]), vn_ref[...], m, acc)
        for pair in range(N_HEADS // 2):
            o_ref[:, pair * LANES:(pair + 1) * LANES] = _pair_output(acc_f[2 * pair], acc_f[2 * pair + 1])


def _sample_attn_call(q, cache_c, cache_kr, layer, wk_t, wv, kg_t, kn, vn, batch, n_new, past, tk=1024):
    tk = min(tk, past)
    nk = past // tk
    new_spec = pl.BlockSpec((N_HEADS, n_new, HEAD_PAD), lambda b, j: (0, b, 0))
    cache = lambda w: pl.BlockSpec((None, None, tk, w), lambda b, j: (layer, b, j, 0))
    return pl.pallas_call(
        _sample_attn_kernel, grid=(batch, nk),
        in_specs=[new_spec, cache(KV_LORA), cache(ROPE_DIM), _const_spec(wk_t.shape),
                  _const_spec(wv.shape), _const_spec((NOPE_DIM, tk)), new_spec, new_spec],
        out_specs=pl.BlockSpec((n_new, D_MODEL), lambda b, j: (b, 0)),
        out_shape=jax.ShapeDtypeStruct((batch * n_new, D_MODEL), BF16),
        scratch_shapes=[pltpu.VMEM((N_HEADS, HEAD_PAD, tk), BF16),
                        pltpu.VMEM((N_HEADS, tk, HEAD_PAD), BF16),
                        pltpu.VMEM((N_HEADS, n_new, 1), F32),
                        pltpu.VMEM((N_HEADS, n_new, HEAD_PAD), F32)],
        compiler_params=_params(2), name="sample_attention")(
            q, cache_c, cache_kr, wk_t, wv, jnp.broadcast_to(kg_t, (NOPE_DIM, tk)), kn, vn)


def _lane_vec(parts):
    v = jnp.concatenate(parts)
    return jnp.pad(v, (0, LANES - v.shape[0])).reshape(1, LANES)


def _rope_tables(pos):
    inv = ROPE_BASE ** (-jnp.arange(HALF_ROPE, dtype=F32) / HALF_ROPE)
    ang = pos.astype(F32)[:, None] * inv[None, :]
    cos, sin = jnp.cos(ang), jnp.sin(ang)
    n = pos.shape[0]
    zeros = lambda w: jnp.zeros((n, w), F32)
    cos_t = jnp.concatenate([jnp.ones((n, NOPE_DIM), F32), cos, cos, zeros(ROPE_TAIL)], axis=1)
    sin_t = jnp.concatenate([zeros(ROPE_LANE0), -sin, sin, zeros(ROPE_TAIL)], axis=1)
    return cos_t, sin_t, cos.T, sin.T


def _prep_mla(w_in, w_uq, w_ukv, q_nope_g, q_rope_g, k_nope_g, k_rope_g):
    zeros = lambda n: jnp.zeros((n,), F32)
    swap = lambda g: jnp.concatenate([g[HALF_ROPE:], g[:HALF_ROPE]])
    w_q, w_kv = w_in[:, :Q_LORA], w_in[:, Q_LORA:Q_LORA + KV_LORA]
    w_kr1 = w_in[:, Q_LORA + KV_LORA:Q_LORA + KV_LORA + HALF_ROPE]
    w_kr2 = w_in[:, Q_LORA + KV_LORA + HALF_ROPE:]
    zc = lambda n: jnp.zeros((D_MODEL, n), F32)
    win = jnp.concatenate([w_q, w_kv, zc(ROPE_LANE0), w_kr1, w_kr2, zc(ROPE_TAIL),
                           zc(ROPE_LANE0), w_kr2, w_kr1, zc(ROPE_TAIL)], axis=1).astype(BF16)
    gains = dict(krg=_lane_vec([zeros(ROPE_LANE0), k_rope_g]),
                 krgs=_lane_vec([zeros(ROPE_LANE0), swap(k_rope_g)]),
                 qhg=_lane_vec([q_nope_g, q_rope_g]) * (ATTN_SCALE * LOG2E),
                 qhgs=_lane_vec([zeros(ROPE_LANE0), swap(q_rope_g)]) * (ATTN_SCALE * LOG2E))

    uq = w_uq.reshape(Q_LORA, N_HEADS, QK_DIM)
    uq_n, uq_1, uq_2 = uq[..., :NOPE_DIM], uq[..., NOPE_DIM:NOPE_DIM + HALF_ROPE], uq[..., NOPE_DIM + HALF_ROPE:]
    zq = lambda n: jnp.zeros((Q_LORA, N_HEADS, n), F32)
    main = jnp.concatenate([uq_n, uq_1, uq_2, zq(ROPE_TAIL)], axis=-1).reshape(Q_LORA, Q_PAD)
    swapped = jnp.concatenate([zq(ROPE_LANE0), uq_2, uq_1, zq(ROPE_TAIL)], axis=-1).reshape(Q_LORA, Q_PAD)
    wuq = jnp.concatenate([main, swapped], axis=1).astype(BF16)
    wuq_t = main.T.astype(BF16)

    ukv = w_ukv.reshape(KV_LORA, N_HEADS // 2, 2, NOPE_DIM + V_DIM)
    zk = jnp.zeros((KV_LORA, N_HEADS // 2, 2, HEAD_PAD - NOPE_DIM), F32)
    wk = jnp.concatenate([ukv[..., :NOPE_DIM], zk], axis=-1).reshape(KV_LORA, Q_PAD).astype(BF16)
    wk_t = ukv[..., :NOPE_DIM].reshape(KV_LORA, N_HEADS * NOPE_DIM).T.astype(BF16)
    zv = jnp.zeros((KV_LORA, N_HEADS // 2, V_DIM), F32)
    v_even = jnp.concatenate([ukv[:, :, 0, NOPE_DIM:], zv], axis=-1)
    v_odd = jnp.concatenate([zv, ukv[:, :, 1, NOPE_DIM:]], axis=-1)
    wv = jnp.stack([v_even, v_odd], axis=2).reshape(KV_LORA, Q_PAD).astype(BF16)
    gains.update(q_nope=q_nope_g * (ATTN_SCALE * LOG2E), q_rope=q_rope_g * (ATTN_SCALE * LOG2E))
    gains.update(k_nope_col=k_nope_g[:, None])
    return win, wuq, wuq_t, wk, wk_t, wv, _lane_vec([k_nope_g]), gains


def _key_tables(cos_t, sin_t, gains):
    return cos_t * gains["krg"], sin_t * gains["krgs"]


def _query_tables(cos_t, sin_t, gains):
    return cos_t * gains["qhg"], sin_t * gains["qhgs"]


def _query_tables_t(cos_tt, sin_tt, gains, tm):
    g1, g2 = gains["q_rope"][:HALF_ROPE, None], gains["q_rope"][HALF_ROPE:, None]
    rot = jnp.concatenate([g1 * cos_tt, g2 * sin_tt, g2 * cos_tt, g1 * sin_tt], axis=0)
    return jnp.broadcast_to(gains["q_nope"][:, None], (NOPE_DIM, tm)), rot


def kernel(x_prompt, x_sample, cache_kv_latent, cache_k_rope, mix_norm_g, mlp_norm_g, sgu_w_in, sgu_norm_g, sgu_w_s, sgu_b_s, sgu_w_out, mla_w_in, mla_q_norm_g, mla_kv_norm_g, mla_w_uq, mla_w_ukv, mla_q_nope_g, mla_q_rope_g, mla_k_nope_g, mla_k_rope_g, mla_w_o, mlp_w_up, mlp_w_down):
    batch, seq, _ = x_prompt.shape
    dec_batch, n_new, _ = x_sample.shape
    past = cache_kv_latent.shape[2]
    depth = mix_norm_g.shape[0]
    rows_p, rows_s = batch * seq, dec_batch * n_new

    xp = x_prompt.reshape(rows_p, D_MODEL)
    xs = x_sample.reshape(rows_s, D_MODEL)
    cos_p, sin_p, cos_pt, sin_pt = _rope_tables(jnp.arange(seq))
    cos_s, sin_s, _, _ = _rope_tables(past + jnp.arange(n_new))
    cos_s, sin_s = jnp.tile(cos_s, (dec_batch, 1)), jnp.tile(sin_s, (dec_batch, 1))
    mlp_wu, mlp_wd = mlp_w_up.astype(BF16), mlp_w_down.astype(BF16)
    sgu_win, sgu_wout = sgu_w_in.astype(BF16), sgu_w_out.astype(BF16)
    mla_wo = mla_w_o.astype(BF16)
    reps = SGU_CHUNK // n_new
    sgu_ws_s = jnp.tile(sgu_w_s[:, :, :n_new, :n_new], (1, 1, reps, reps))
    sgu_b_p = sgu_b_s[..., None]
    sgu_b_smp = jnp.tile(sgu_b_s[:, :, :n_new], (1, 1, reps))[..., None]

    n_mla = mla_w_in.shape[0]
    cache_p = cache_s = None
    sgu_v_s = []
    for i in range(depth):
        j = i // 2
        g_mix = mix_norm_g[i].reshape(1, D_MODEL)
        g_mlp = mlp_norm_g[i].reshape(1, D_MODEL)
        if i % 2 == 0:
            ng = sgu_norm_g[j].reshape(1, D_SGU)
            xp = _sgu_call(xp, g_mix, sgu_win, ng, sgu_w_s, sgu_b_p, sgu_wout, j,
                           block_diag=False, emit_v=False)
            xs, v_new = _sgu_call(xs, g_mix, sgu_win, ng, sgu_ws_s, sgu_b_smp, sgu_wout, j,
                                  block_diag=True, emit_v=True)
            sgu_v_s.append(v_new.reshape(dec_batch, n_new, D_SGU))
            xp = _mlp_call(xp, g_mlp, mlp_wu, mlp_wd, i)
            xs = _mlp_call(xs, g_mlp, mlp_wu, mlp_wd, i)
        else:
            win, wuq, wuq_t, wk, wk_t, wv, kg, gains = _prep_mla(
                mla_w_in[j], mla_w_uq[j], mla_w_ukv[j], mla_q_nope_g[j], mla_q_rope_g[j],
                mla_k_nope_g[j], mla_k_rope_g[j])
            qg, kvg = mla_q_norm_g[j].reshape(1, -1), mla_kv_norm_g[j].reshape(1, -1)
            *cache_p, q_t, k, vt = _mla_proj_call(
                xp, g_mix, win, qg, kvg, _key_tables(cos_p, sin_p, gains), wuq_t,
                _query_tables_t(cos_pt, sin_pt, gains, min(ATTN_BLOCK, seq)), j, n_mla, cache_p,
                kv_weights=(wk, wv.T, kg))
            attn = _prompt_attn_call(q_t, k, vt, batch, seq)
            xp = _mlp_call(xp, g_mlp, mlp_wu, mlp_wd, i, attn=attn, wo=mla_wo, wo_layer=j)
            *cache_s, krpad, q = _mla_proj_call(
                xs, g_mix, win, qg, kvg, _key_tables(cos_s, sin_s, gains), wuq,
                _query_tables(cos_s, sin_s, gains), j, n_mla, cache_s)
            kn, vn = _kv_expand_call(cache_s[0], j, krpad, wk, wv, kg)
            attn = _sample_attn_call(q, cache_kv_latent, cache_k_rope, j, wk_t, wv, gains["k_nope_col"],
                                     kn, vn, dec_batch, n_new, past)
            xs = _mlp_call(xs, g_mlp, mlp_wu, mlp_wd, i, attn=attn, wo=mla_wo, wo_layer=j)
    return (xp.reshape(batch, seq, D_MODEL), xs.reshape(dec_batch, n_new, D_MODEL),
            cache_p[0].reshape(n_mla, batch, seq, KV_LORA), cache_p[1].reshape(n_mla, batch, seq, ROPE_DIM),
            cache_s[0].reshape(n_mla, dec_batch, n_new, KV_LORA),
            cache_s[1].reshape(n_mla, dec_batch, n_new, ROPE_DIM), jnp.stack(sgu_v_s))
```

```python
import functools
import math

import jax
import jax.numpy as jnp
import numpy as np
from jax import lax
from jax.experimental import pallas as pl
from jax.experimental.pallas import tpu as pltpu

F32 = jnp.float32
BF16 = jnp.bfloat16

D_MODEL = 1024
CHUNK = 64
SGU_CHUNK = 128
D_SGU = 2 * D_MODEL
SGU_GROUP_DIM = 128
SGU_GROUPS = D_SGU // SGU_GROUP_DIM
N_HEADS = 16
Q_LORA = 384
KV_LORA = 256
NOPE_DIM = 64
ROPE_DIM = 32
HALF_ROPE = ROPE_DIM // 2
V_DIM = 64
QK_DIM = NOPE_DIM + ROPE_DIM
ATTN_SCALE = 1.0 / math.sqrt(QK_DIM)
ROPE_BASE = 10000.0
D_FF = 4 * D_MODEL
EPS = 1e-6
NEG_INF = -1e30

LANES = 128
HEAD_PAD = LANES
FF_CHUNK = 1024
SGU_COL_CHUNK = 512
VMEM_LIMIT = 56 * 1024 * 1024
ATTN_BLOCK = 512


def _params(n_axes):
    return pltpu.CompilerParams(
        dimension_semantics=("arbitrary",) * n_axes, vmem_limit_bytes=VMEM_LIMIT)


def _const_spec(shape):
    nd = len(shape)
    return pl.BlockSpec(shape, lambda *_: (0,) * nd, pipeline_mode=pl.Buffered(1))


def _layer_spec(stack, layer):
    shape = stack.shape[1:]
    return pl.BlockSpec((None,) + shape, lambda *_: (layer,) + (0,) * len(shape),
                        pipeline_mode=pl.Buffered(1))


def _row_tile(rows, want):
    tm = min(rows, want)
    assert rows % tm == 0
    return tm


def _rms(xf, g, n):
    ss = jnp.sum(xf * xf, axis=-1, keepdims=True)
    return xf * lax.rsqrt(ss * (1.0 / n) + EPS) * g


def _gelu(z):
    return 0.5 * z * (1.0 + lax.erf(z * np.float32(math.sqrt(0.5))))


def _mlp_delta(x, g_ref, wu_ref, wd_ref):
    h = _rms(x, g_ref[...], D_MODEL).astype(BF16)
    acc = None
    for f in range(D_FF // FF_CHUNK):
        sl = slice(f * FF_CHUNK, (f + 1) * FF_CHUNK)
        a = jnp.dot(h, wu_ref[:, sl], preferred_element_type=F32)
        a = jnp.square(jnp.maximum(a, 0.0)).astype(BF16)
        d = jnp.dot(a, wd_ref[sl, :], preferred_element_type=F32)
        acc = d if acc is None else acc + d
    return acc


def _mlp_kernel(x_ref, g_ref, wu_ref, wd_ref, o_ref):
    x = x_ref[...]
    o_ref[...] = x + _mlp_delta(x, g_ref, wu_ref, wd_ref)


def _attn_out_mlp_kernel(x_ref, a_ref, wo_ref, g_ref, wu_ref, wd_ref, o_ref):
    x = x_ref[...] + jnp.dot(a_ref[...], wo_ref[...], preferred_element_type=F32)
    o_ref[...] = x + _mlp_delta(x, g_ref, wu_ref, wd_ref)


def _mlp_call(x, g, wu, wd, layer, attn=None, wo=None, wo_layer=None, tm_want=512):
    rows = x.shape[0]
    tm = _row_tile(rows, tm_want)
    row_spec = pl.BlockSpec((tm, D_MODEL), lambda i: (i, 0))
    w_specs = [_const_spec((1, D_MODEL)), _layer_spec(wu, layer), _layer_spec(wd, layer)]
    if attn is None:
        kern, ins, specs = _mlp_kernel, (x, g, wu, wd), [row_spec] + w_specs
        name = "channel_mlp"
    else:
        kern, ins = _attn_out_mlp_kernel, (x, attn, wo, g, wu, wd)
        specs = [row_spec, row_spec, _layer_spec(wo, wo_layer)] + w_specs
        name = "attn_out_channel_mlp"
    return pl.pallas_call(
        kern, grid=(rows // tm,), in_specs=specs, out_specs=row_spec,
        out_shape=jax.ShapeDtypeStruct((rows, D_MODEL), F32),
        compiler_params=_params(1), name=name)(*ins)


def _sgu_kernel(x_ref, g_ref, win_ref, ng_ref, ws_ref, b_ref, wout_ref, *rest, block_diag, emit_v):
    if emit_v:
        o_ref, v_ref, u_scr, v_scr, y_scr = rest
    else:
        o_ref, u_scr, v_scr, y_scr = rest
        v_ref = None
    tm = x_ref.shape[0]
    n_chunks = tm // SGU_CHUNK
    x = x_ref[...]
    h = _rms(x, g_ref[...], D_MODEL).astype(BF16)

    ss = jnp.zeros((tm, 1), F32)
    for k in range(D_SGU // SGU_COL_CHUNK):
        sl = slice(k * SGU_COL_CHUNK, (k + 1) * SGU_COL_CHUNK)
        u_scr[:, sl] = _gelu(jnp.dot(h, win_ref[:, sl], preferred_element_type=F32))
        slv = slice(D_SGU + k * SGU_COL_CHUNK, D_SGU + (k + 1) * SGU_COL_CHUNK)
        zv = _gelu(jnp.dot(h, win_ref[:, slv], preferred_element_type=F32))
        ss = ss + jnp.sum(zv * zv, axis=-1, keepdims=True)
        v_scr[:, sl] = zv
    r = lax.rsqrt(ss * (1.0 / D_SGU) + EPS)

    ri = lax.broadcasted_iota(jnp.int32, (SGU_CHUNK, SGU_CHUNK), 0) // CHUNK
    ci = lax.broadcasted_iota(jnp.int32, (SGU_CHUNK, SGU_CHUNK), 1) // CHUNK
    mask = (ri == ci) if block_diag else (ri >= ci)
    for grp in range(SGU_GROUPS):
        gsl = slice(grp * SGU_GROUP_DIM, (grp + 1) * SGU_GROUP_DIM)
        ws = jnp.where(mask, ws_ref[grp], 0.0).astype(BF16)
        pieces = []
        for c in range(n_chunks):
            rsl = slice(c * SGU_CHUNK, (c + 1) * SGU_CHUNK)
            vn = v_scr[rsl, gsl] * r[rsl] * ng_ref[:, gsl]
            if emit_v:
                v_ref[rsl, gsl] = vn
            pieces.append(vn.astype(BF16))
        rhs = pieces[0] if n_chunks == 1 else jnp.concatenate(pieces, axis=1)
        mixed = jnp.dot(ws, rhs, preferred_element_type=F32) + b_ref[grp]
        for c in range(n_chunks):
            rsl = slice(c * SGU_CHUNK, (c + 1) * SGU_CHUNK)
            y = u_scr[rsl, gsl] * mixed[:, c * SGU_CHUNK:(c + 1) * SGU_CHUNK]
            y_scr[rsl, gsl] = y.astype(BF16)
    o_ref[...] = x + jnp.dot(y_scr[...], wout_ref[...], preferred_element_type=F32)


def _sgu_call(x, g, win, ng, ws, b, wout, layer, *, block_diag, emit_v, tm_want=512):
    rows = x.shape[0]
    tm = _row_tile(rows, tm_want)
    row_spec = pl.BlockSpec((tm, D_MODEL), lambda i: (i, 0))
    specs = [row_spec, _const_spec((1, D_MODEL)), _layer_spec(win, layer), _const_spec((1, D_SGU)),
             _layer_spec(ws, layer), _layer_spec(b, layer), _layer_spec(wout, layer)]
    out_shape = [jax.ShapeDtypeStruct((rows, D_MODEL), F32)]
    out_specs = [row_spec]
    if emit_v:
        out_shape.append(jax.ShapeDtypeStruct((rows, D_SGU), F32))
        out_specs.append(pl.BlockSpec((tm, D_SGU), lambda i: (i, 0)))
    res = pl.pallas_call(
        functools.partial(_sgu_kernel, block_diag=block_diag, emit_v=emit_v),
        grid=(rows // tm,), in_specs=specs, out_specs=out_specs, out_shape=out_shape,
        scratch_shapes=[pltpu.VMEM((tm, D_SGU), F32), pltpu.VMEM((tm, D_SGU), F32),
                        pltpu.VMEM((tm, D_SGU), BF16)],
        compiler_params=_params(1), name="sgu_mixer_v" if emit_v else "sgu_mixer")(
            x, g, win, ng, ws, b, wout)
    return res if emit_v else res[0]


ROPE_LANE0 = NOPE_DIM
ROPE_TAIL = HEAD_PAD - ROPE_LANE0 - ROPE_DIM
W_IN_PAD = Q_LORA + KV_LORA + 2 * LANES
Q_PAD = N_HEADS * HEAD_PAD
LOG2E = math.log2(math.e)


def _ones_lane(hd):
    return V_DIM if hd % 2 == 0 else 0


def _mla_proj_kernel(x_ref, g_ref, win_ref, qg_ref, kvg_ref, kcos_ref, ksin_ref, wuq_ref,
                     qtab_a_ref, qtab_b_ref, *rest, prompt_layout):
    if prompt_layout:
        wk_ref, wvt_ref, kg_ref = rest[:3]
        ckv_ref, kr_ref, q_ref, k_ref, vt_ref = rest[-5:]
    else:
        ckv_ref, kr_ref, krpad_ref, q_ref = rest[-4:]
    x = x_ref[...]
    h = _rms(x, g_ref[...], D_MODEL).astype(BF16)
    a = jnp.dot(h, win_ref[...], preferred_element_type=F32)
    c_q = _rms(a[:, :Q_LORA], qg_ref[...], Q_LORA).astype(BF16)
    c_kv = _rms(a[:, Q_LORA:Q_LORA + KV_LORA], kvg_ref[...], KV_LORA)
    ckv_ref[...] = c_kv

    kr = a[:, Q_LORA + KV_LORA:Q_LORA + KV_LORA + LANES]
    kr_sw = a[:, Q_LORA + KV_LORA + LANES:]
    r_kr = lax.rsqrt(jnp.sum(kr * kr, axis=-1, keepdims=True) * (1.0 / ROPE_DIM) + EPS)
    kr_rot = (kr * r_kr) * kcos_ref[...] + (kr_sw * r_kr) * ksin_ref[...]
    kr_ref[...] = kr_rot[:, ROPE_LANE0:ROPE_LANE0 + ROPE_DIM]

    if prompt_layout:
        c = c_kv.astype(BF16)
        _expand_keys(c, kr_rot, wk_ref, kg_ref, k_ref)
        _expand_values_t(c, wvt_ref, vt_ref)
        q_t = lax.dot_general(wuq_ref[...], c_q, (((1,), (1,)), ((), ())), preferred_element_type=F32)
        g_nope = qtab_a_ref[...]
        rot = qtab_b_ref[...]
        t_a, t_b, t_c, t_d = (rot[i * HALF_ROPE:(i + 1) * HALF_ROPE] for i in range(4))
        pad = jnp.zeros((ROPE_TAIL, q_t.shape[1]), F32)
        for hd in range(N_HEADS):
            base = hd * HEAD_PAD
            nope = q_t[base:base + NOPE_DIM]
            x1 = q_t[base + NOPE_DIM:base + NOPE_DIM + HALF_ROPE]
            x2 = q_t[base + NOPE_DIM + HALF_ROPE:base + QK_DIM]
            s_n = jnp.sum(nope * nope, axis=0, keepdims=True)
            s_r = jnp.sum(x1 * x1 + x2 * x2, axis=0, keepdims=True)
            r_n = lax.rsqrt(s_n * (1.0 / NOPE_DIM) + EPS)
            r_r = lax.rsqrt(s_r * (1.0 / ROPE_DIM) + EPS)
            x1, x2 = x1 * r_r, x2 * r_r
            q_ref[hd] = jnp.concatenate(
                [nope * r_n * g_nope, x1 * t_a - x2 * t_b, x2 * t_c + x1 * t_d, pad], axis=0).astype(BF16)
        return

    krpad_ref[...] = kr_rot.astype(BF16)
    qq = jnp.dot(c_q, wuq_ref[...], preferred_element_type=F32)
    qcos = qtab_a_ref[...]
    qsin = qtab_b_ref[...]
    is_nope = lax.broadcasted_iota(jnp.int32, (1, HEAD_PAD), 1) < NOPE_DIM
    for hd in range(N_HEADS):
        blk = qq[:, hd * HEAD_PAD:(hd + 1) * HEAD_PAD]
        blk_sw = qq[:, Q_PAD + hd * HEAD_PAD:Q_PAD + (hd + 1) * HEAD_PAD]
        sq = blk * blk
        s_n = jnp.sum(jnp.where(is_nope, sq, 0.0), axis=-1, keepdims=True)
        s_r = jnp.sum(jnp.where(is_nope, 0.0, sq), axis=-1, keepdims=True)
        r_n = lax.rsqrt(s_n * (1.0 / NOPE_DIM) + EPS)
        r_r = lax.rsqrt(s_r * (1.0 / ROPE_DIM) + EPS)
        q_ref[hd] = ((blk * jnp.where(is_nope, r_n, r_r)) * qcos + (blk_sw * r_r) * qsin).astype(BF16)


def _mla_proj_call(x, g, win, qg, kvg, k_tables, wuq, q_tables, layer, n_layers, prev=None, *,
                   kv_weights=None, tm_want=ATTN_BLOCK):
    prompt_layout = kv_weights is not None
    rows = x.shape[0]
    period = k_tables[0].shape[0]
    tm = _row_tile(min(rows, period), tm_want)
    n_per = period // tm
    row = lambda w: pl.BlockSpec((tm, w), lambda i: (i, 0))
    slot = lambda w: pl.BlockSpec((None, tm, w), lambda i: (layer, i, 0))
    tab = pl.BlockSpec((tm, HEAD_PAD), lambda i: (i % n_per, 0))
    head = pl.BlockSpec((N_HEADS, tm, HEAD_PAD), lambda i: (0, i, 0))
    head_shape = jax.ShapeDtypeStruct((N_HEADS, rows, HEAD_PAD), BF16)
    specs = [row(D_MODEL), _const_spec((1, D_MODEL)), _const_spec((D_MODEL, W_IN_PAD)),
             _const_spec((1, Q_LORA)), _const_spec((1, KV_LORA)), tab, tab, _const_spec(wuq.shape)]
    ins = [x, g, win, qg, kvg, *k_tables, wuq, *q_tables]
    out_shape = [jax.ShapeDtypeStruct((n_layers, rows, KV_LORA), F32),
                 jax.ShapeDtypeStruct((n_layers, rows, ROPE_DIM), F32)]
    out_specs = [slot(KV_LORA), slot(ROPE_DIM)]
    if prompt_layout:
        specs += [_const_spec((NOPE_DIM, tm)), pl.BlockSpec((2 * ROPE_DIM, tm), lambda i: (0, i % n_per))]
        specs += [_const_spec(w.shape) for w in kv_weights]
        ins += list(kv_weights)
        out_shape += [jax.ShapeDtypeStruct((N_HEADS, HEAD_PAD, rows), BF16), head_shape,
                      jax.ShapeDtypeStruct((N_HEADS, rows // tm, HEAD_PAD, tm), BF16)]
        out_specs += [pl.BlockSpec((N_HEADS, HEAD_PAD, tm), lambda i: (0, 0, i)), head,
                      pl.BlockSpec((N_HEADS, 1, HEAD_PAD, tm), lambda i: (0, i, 0, 0))]
    else:
        specs += [tab, tab]
        out_shape += [jax.ShapeDtypeStruct((rows, LANES), BF16), head_shape]
        out_specs += [row(LANES), head]
    aliases = {}
    if prev is not None:
        aliases = {len(ins): 0, len(ins) + 1: 1}
        ins += list(prev)
        specs += [pl.BlockSpec(memory_space=pl.ANY)] * 2
    return pl.pallas_call(
        functools.partial(_mla_proj_kernel, prompt_layout=prompt_layout), grid=(rows // tm,),
        in_specs=specs, out_specs=out_specs, out_shape=out_shape, input_output_aliases=aliases,
        compiler_params=_params(1), name="mla_project_kv" if prompt_layout else "mla_project")(*ins)


def _expand_keys(c, krp, wk_ref, kg_ref, k_dst):
    kk = jnp.dot(c, wk_ref[...], preferred_element_type=F32)
    for hd in range(N_HEADS):
        blk = kk[:, hd * HEAD_PAD:(hd + 1) * HEAD_PAD]
        r = lax.rsqrt(jnp.sum(blk * blk, axis=-1, keepdims=True) * (1.0 / NOPE_DIM) + EPS)
        k_dst[hd] = (blk * r * kg_ref[...] + krp).astype(BF16)


def _expand_values(c, wv_ref, v_dst):
    vv = jnp.dot(c, wv_ref[...], preferred_element_type=F32)
    lane = lax.broadcasted_iota(jnp.int32, (1, HEAD_PAD), 1)
    for hd in range(N_HEADS):
        vblk = vv[:, hd * HEAD_PAD:(hd + 1) * HEAD_PAD]
        v_dst[hd] = jnp.where(lane == _ones_lane(hd), 1.0, vblk).astype(BF16)


def _expand_values_t(c, wvt_ref, vt_dst):
    vt = lax.dot_general(wvt_ref[...], c, (((1,), (1,)), ((), ())), preferred_element_type=F32)
    row = lax.broadcasted_iota(jnp.int32, (HEAD_PAD, 1), 0)
    for hd in range(N_HEADS):
        blk = vt[hd * HEAD_PAD:(hd + 1) * HEAD_PAD, :]
        vt_dst[hd, 0] = jnp.where(row == _ones_lane(hd), 1.0, blk).astype(BF16)


def _kv_expand_kernel(c_ref, krpad_ref, wk_ref, wv_ref, kg_ref, k_ref, v_ref):
    c = c_ref[...].astype(BF16)
    _expand_keys(c, krpad_ref[...].astype(F32), wk_ref, kg_ref, k_ref)
    _expand_values(c, wv_ref, v_ref)


def _kv_expand_call(c_stack, layer, krpad, wk, wv, kg, tm_want=512):
    rows = c_stack.shape[1]
    tm = _row_tile(rows, tm_want)
    row = lambda w: pl.BlockSpec((tm, w), lambda i: (i, 0))
    c_spec = pl.BlockSpec((None, tm, KV_LORA), lambda i: (layer, i, 0))
    head = pl.BlockSpec((N_HEADS, tm, HEAD_PAD), lambda i: (0, i, 0))
    hs = jax.ShapeDtypeStruct((N_HEADS, rows, HEAD_PAD), BF16)
    return pl.pallas_call(
        _kv_expand_kernel, grid=(rows // tm,),
        in_specs=[c_spec, row(LANES), _const_spec(wk.shape), _const_spec(wv.shape),
                  _const_spec((1, HEAD_PAD))],
        out_specs=[head, head], out_shape=[hs, hs],
        compiler_params=_params(1), name="kv_expand")(c_stack, krpad, wk, wv, kg)


def _scores(q, k):
    batch_dims = tuple(range(q.ndim - 2))
    return lax.dot_general(q, k, (((q.ndim - 1,), (k.ndim - 1,)), (batch_dims, batch_dims)),
                           preferred_element_type=F32)


def _online_update(s, v, m, acc):
    m_new = jnp.maximum(m, jnp.max(s, axis=-1, keepdims=True))
    alpha = jnp.exp2(m - m_new)
    p = jnp.exp2(s - m_new).astype(BF16)
    batch_dims = tuple(range(p.ndim - 2))
    pv = lax.dot_general(p, v, (((p.ndim - 1,), (v.ndim - 2,)), (batch_dims, batch_dims)),
                         preferred_element_type=F32)
    return m_new, alpha * acc + pv


def _pair_output(acc_even, acc_odd):
    l_even = acc_even[:, _ones_lane(0):_ones_lane(0) + 1]
    l_odd = acc_odd[:, _ones_lane(1):_ones_lane(1) + 1]
    lane = lax.broadcasted_iota(jnp.int32, (1, HEAD_PAD), 1)
    return jnp.where(lane < V_DIM, acc_even / l_even, acc_odd / l_odd).astype(BF16)


HEADS_PER_STEP = 4
DIAG_STRIP = 256
SCORE_LOOKAHEAD = 2


def _prompt_attn_kernel(qt_ref, k_ref, vt_ref, o_ref, *, tq):
    seq = k_ref.shape[1]
    row = lax.broadcasted_iota(jnp.int32, (HEAD_PAD, 1), 0)
    heads = range(HEADS_PER_STEP)

    def update_all(ss, vts, carry):
        stats = []
        for s, (m, _) in zip(ss, carry):
            m_new = jnp.maximum(m, jnp.max(s, axis=0, keepdims=True))
            stats.append((m_new, jnp.exp2(m - m_new), jnp.exp2(s - m_new).astype(BF16)))
        return [(m_new, alpha * acc + jnp.dot(vt, p, preferred_element_type=F32))
                for (m_new, alpha, p), vt, (_, acc) in zip(stats, vts, carry)]

    n_strips = tq // DIAG_STRIP

    def scores(qi, step):
        q0 = qi * tq
        if step < qi:
            k_rows, q_cols = slice(step * tq, (step + 1) * tq), slice(q0, q0 + tq)
        else:
            c0 = (step - qi) * DIAG_STRIP
            k_rows, q_cols = slice(q0, q0 + c0 + DIAG_STRIP), slice(q0 + c0, q0 + c0 + DIAG_STRIP)
        return [jnp.dot(k_ref[hd, k_rows, :], qt_ref[hd, :, q_cols], preferred_element_type=F32)
                for hd in heads]

    schedule = sorted(((qi, step) for qi in range(seq // tq) for step in range(qi + n_strips)),
                      key=lambda qs: (qs[1] - qs[0], -qs[0]))
    carries = {}
    pending = [scores(*schedule[i]) for i in range(SCORE_LOOKAHEAD)]
    for t, (qi, step) in enumerate(schedule):
        q0 = qi * tq
        if step == 0:
            carries[qi] = [(jnp.full((1, tq), NEG_INF, F32), jnp.zeros((HEAD_PAD, tq), F32)) for _ in heads]
        carry = carries[qi]
        ss = pending.pop(0)
        if t + SCORE_LOOKAHEAD < len(schedule):
            pending.append(scores(*schedule[t + SCORE_LOOKAHEAD]))
        if step < qi:
            carries[qi] = update_all(ss, [vt_ref[hd, step] for hd in heads], carry)
            continue
        c0 = (step - qi) * DIAG_STRIP
        n_keys = c0 + DIAG_STRIP
        key_chunk = lax.broadcasted_iota(jnp.int32, (n_keys, DIAG_STRIP), 0) // CHUNK
        qry_chunk = (lax.broadcasted_iota(jnp.int32, (n_keys, DIAG_STRIP), 1) + c0) // CHUNK
        mask = key_chunk <= qry_chunk
        strip = update_all([jnp.where(mask, s, NEG_INF) for s in ss],
                           [vt_ref[hd, qi, :, :n_keys] for hd in heads],
                           [(m[:, c0:c0 + DIAG_STRIP], acc[:, c0:c0 + DIAG_STRIP]) for m, acc in carry])
        for p in range(HEADS_PER_STEP // 2):
            acc_e, acc_o = strip[2 * p][1], strip[2 * p + 1][1]
            l_e = acc_e[_ones_lane(0):_ones_lane(0) + 1, :]
            l_o = acc_o[_ones_lane(1):_ones_lane(1) + 1, :]
            o_t = jnp.where(row < V_DIM, acc_e / l_e, acc_o / l_o)
            o_ref[q0 + c0:q0 + n_keys, p * LANES:(p + 1) * LANES] = o_t.T.astype(BF16)


def _prompt_attn_call(q_t, k, vt, batch, seq, tq=ATTN_BLOCK):
    tq = min(tq, seq)
    nq = seq // tq
    rows = batch * seq
    assert vt.shape == (N_HEADS, batch * nq, HEAD_PAD, tq)
    g = HEADS_PER_STEP
    return pl.pallas_call(
        functools.partial(_prompt_attn_kernel, tq=tq),
        grid=(batch, N_HEADS // g),
        in_specs=[pl.BlockSpec((g, HEAD_PAD, seq), lambda b, i: (i, 0, b)),
                  pl.BlockSpec((g, seq, HEAD_PAD), lambda b, i: (i, b, 0)),
                  pl.BlockSpec((g, nq, HEAD_PAD, tq), lambda b, i: (i, b, 0, 0))],
        out_specs=pl.BlockSpec((seq, g // 2 * LANES), lambda b, i: (b, i)),
        out_shape=jax.ShapeDtypeStruct((rows, D_MODEL), BF16),
        compiler_params=_params(2), name="prompt_attention")(q_t, k, vt)


def _expand_cache_block(c_ref, kr_ref, wkt_ref, wv_ref, kgt_ref, kt_dst, v_dst):
    place_t = (lax.broadcasted_iota(jnp.int32, (HEAD_PAD, ROPE_DIM), 0)
               == lax.broadcasted_iota(jnp.int32, (HEAD_PAD, ROPE_DIM), 1) + ROPE_LANE0).astype(BF16)
    contract_last = (((1,), (1,)), ((), ()))
    rope_rows = lax.dot_general(place_t, kr_ref[...].astype(BF16), contract_last,
                                preferred_element_type=F32)[NOPE_DIM:]
    c = c_ref[...].astype(BF16)
    k_t = lax.dot_general(wkt_ref[...], c, contract_last, preferred_element_type=F32)
    for hd in range(N_HEADS):
        blk = k_t[hd * NOPE_DIM:(hd + 1) * NOPE_DIM]
        r = lax.rsqrt(jnp.sum(blk * blk, axis=0, keepdims=True) * (1.0 / NOPE_DIM) + EPS)
        kt_dst[hd] = jnp.concatenate([blk * r * kgt_ref[...], rope_rows], axis=0).astype(BF16)
    _expand_values(c, wv_ref, v_dst)


def _sample_attn_kernel(q_ref, c_ref, kr_ref, wkt_ref, wv_ref, kgt_ref, kn_ref, vn_ref, o_ref,
                        kt_scr, v_scr, m_scr, acc_scr):
    kb = pl.program_id(1)
    last = pl.num_programs(1) - 1

    @pl.when(kb == 0)
    def _():
        m_scr[...] = jnp.full(m_scr.shape, NEG_INF, F32)
        acc_scr[...] = jnp.zeros(acc_scr.shape, F32)

    _expand_cache_block(c_ref, kr_ref, wkt_ref, wv_ref, kgt_ref, kt_scr, v_scr)
    q = q_ref[...]
    s = lax.dot_general(q, kt_scr[...], (((2,), (1,)), ((0,), (0,))), preferred_element_type=F32)
    m, acc = _online_update(s, v_scr[...], m_scr[...], acc_scr[...])
    m_scr[...] = m
    acc_scr[...] = acc

    @pl.when(kb == last)
    def _():
        _, acc_f = _online_update(_scores(q, kn_ref[...]), vn_ref[...], m, acc)
        for pair in range(N_HEADS // 2):
            o_ref[:, pair * LANES:(pair + 1) * LANES] = _pair_output(acc_f[2 * pair], acc_f[2 * pair + 1])


def _sample_attn_call(q, cache_c, cache_kr, layer, wk_t, wv, kg_t, kn, vn, batch, n_new, past, tk=1024):
    tk = min(tk, past)
    nk = past // tk
    new_spec = pl.BlockSpec((N_HEADS, n_new, HEAD_PAD), lambda b, j: (0, b, 0))
    cache = lambda w: pl.BlockSpec((None, None, tk, w), lambda b, j: (layer, b, j, 0))
    return pl.pallas_call(
        _sample_attn_kernel, grid=(batch, nk),
        in_specs=[new_spec, cache(KV_LORA), cache(ROPE_DIM), _const_spec(wk_t.shape),
                  _const_spec(wv.shape), _const_spec((NOPE_DIM, tk)), new_spec, new_spec],
        out_specs=pl.BlockSpec((n_new, D_MODEL), lambda b, j: (b, 0)),
        out_shape=jax.ShapeDtypeStruct((batch * n_new, D_MODEL), BF16),
        scratch_shapes=[pltpu.VMEM((N_HEADS, HEAD_PAD, tk), BF16),
                        pltpu.VMEM((N_HEADS, tk, HEAD_PAD), BF16),
                        pltpu.VMEM((N_HEADS, n_new, 1), F32),
                        pltpu.VMEM((N_HEADS, n_new, HEAD_PAD), F32)],
        compiler_params=_params(2), name="sample_attention")(
            q, cache_c, cache_kr, wk_t, wv, jnp.broadcast_to(kg_t, (NOPE_DIM, tk)), kn, vn)


def _lane_vec(parts):
    v = jnp.concatenate(parts)
    return jnp.pad(v, (0, LANES - v.shape[0])).reshape(1, LANES)


def _rope_tables(pos):
    inv = ROPE_BASE ** (-jnp.arange(HALF_ROPE, dtype=F32) / HALF_ROPE)
    ang = pos.astype(F32)[:, None] * inv[None, :]
    cos, sin = jnp.cos(ang), jnp.sin(ang)
    n = pos.shape[0]
    zeros = lambda w: jnp.zeros((n, w), F32)
    cos_t = jnp.concatenate([jnp.ones((n, NOPE_DIM), F32), cos, cos, zeros(ROPE_TAIL)], axis=1)
    sin_t = jnp.concatenate([zeros(ROPE_LANE0), -sin, sin, zeros(ROPE_TAIL)], axis=1)
    return cos_t, sin_t, cos.T, sin.T


def _prep_mla(w_in, w_uq, w_ukv, q_nope_g, q_rope_g, k_nope_g, k_rope_g):
    zeros = lambda n: jnp.zeros((n,), F32)
    swap = lambda g: jnp.concatenate([g[HALF_ROPE:], g[:HALF_ROPE]])
    w_q, w_kv = w_in[:, :Q_LORA], w_in[:, Q_LORA:Q_LORA + KV_LORA]
    w_kr1 = w_in[:, Q_LORA + KV_LORA:Q_LORA + KV_LORA + HALF_ROPE]
    w_kr2 = w_in[:, Q_LORA + KV_LORA + HALF_ROPE:]
    zc = lambda n: jnp.zeros((D_MODEL, n), F32)
    win = jnp.concatenate([w_q, w_kv, zc(ROPE_LANE0), w_kr1, w_kr2, zc(ROPE_TAIL),
                           zc(ROPE_LANE0), w_kr2, w_kr1, zc(ROPE_TAIL)], axis=1).astype(BF16)
    gains = dict(krg=_lane_vec([zeros(ROPE_LANE0), k_rope_g]),
                 krgs=_lane_vec([zeros(ROPE_LANE0), swap(k_rope_g)]),
                 qhg=_lane_vec([q_nope_g, q_rope_g]) * (ATTN_SCALE * LOG2E),
                 qhgs=_lane_vec([zeros(ROPE_LANE0), swap(q_rope_g)]) * (ATTN_SCALE * LOG2E))

    uq = w_uq.reshape(Q_LORA, N_HEADS, QK_DIM)
    uq_n, uq_1, uq_2 = uq[..., :NOPE_DIM], uq[..., NOPE_DIM:NOPE_DIM + HALF_ROPE], uq[..., NOPE_DIM + HALF_ROPE:]
    zq = lambda n: jnp.zeros((Q_LORA, N_HEADS, n), F32)
    main = jnp.concatenate([uq_n, uq_1, uq_2, zq(ROPE_TAIL)], axis=-1).reshape(Q_LORA, Q_PAD)
    swapped = jnp.concatenate([zq(ROPE_LANE0), uq_2, uq_1, zq(ROPE_TAIL)], axis=-1).reshape(Q_LORA, Q_PAD)
    wuq = jnp.concatenate([main, swapped], axis=1).astype(BF16)
    wuq_t = main.T.astype(BF16)

    ukv = w_ukv.reshape(KV_LORA, N_HEADS // 2, 2, NOPE_DIM + V_DIM)
    zk = jnp.zeros((KV_LORA, N_HEADS // 2, 2, HEAD_PAD - NOPE_DIM), F32)
    wk = jnp.concatenate([ukv[..., :NOPE_DIM], zk], axis=-1).reshape(KV_LORA, Q_PAD).astype(BF16)
    wk_t = ukv[..., :NOPE_DIM].reshape(KV_LORA, N_HEADS * NOPE_DIM).T.astype(BF16)
    zv = jnp.zeros((KV_LORA, N_HEADS // 2, V_DIM), F32)
    v_even = jnp.concatenate([ukv[:, :, 0, NOPE_DIM:], zv], axis=-1)
    v_odd = jnp.concatenate([zv, ukv[:, :, 1, NOPE_DIM:]], axis=-1)
    wv = jnp.stack([v_even, v_odd], axis=2).reshape(KV_LORA, Q_PAD).astype(BF16)
    gains.update(q_nope=q_nope_g * (ATTN_SCALE * LOG2E), q_rope=q_rope_g * (ATTN_SCALE * LOG2E))
    gains.update(k_nope_col=k_nope_g[:, None])
    return win, wuq, wuq_t, wk, wk_t, wv, _lane_vec([k_nope_g]), gains


def _key_tables(cos_t, sin_t, gains):
    return cos_t * gains["krg"], sin_t * gains["krgs"]


def _query_tables(cos_t, sin_t, gains):
    return cos_t * gains["qhg"], sin_t * gains["qhgs"]


def _query_tables_t(cos_tt, sin_tt, gains, tm):
    g1, g2 = gains["q_rope"][:HALF_ROPE, None], gains["q_rope"][HALF_ROPE:, None]
    rot = jnp.concatenate([g1 * cos_tt, g2 * sin_tt, g2 * cos_tt, g1 * sin_tt], axis=0)
    return jnp.broadcast_to(gains["q_nope"][:, None], (NOPE_DIM, tm)), rot


def kernel(x_prompt, x_sample, cache_kv_latent, cache_k_rope, mix_norm_g, mlp_norm_g, sgu_w_in, sgu_norm_g, sgu_w_s, sgu_b_s, sgu_w_out, mla_w_in, mla_q_norm_g, mla_kv_norm_g, mla_w_uq, mla_w_ukv, mla_q_nope_g, mla_q_rope_g, mla_k_nope_g, mla_k_rope_g, mla_w_o, mlp_w_up, mlp_w_down):
    batch, seq, _ = x_prompt.shape
    dec_batch, n_new, _ = x_sample.shape
    past = cache_kv_latent.shape[2]
    depth = mix_norm_g.shape[0]
    rows_p, rows_s = batch * seq, dec_batch * n_new

    xp = x_prompt.reshape(rows_p, D_MODEL)
    xs = x_sample.reshape(rows_s, D_MODEL)
    cos_p, sin_p, cos_pt, sin_pt = _rope_tables(jnp.arange(seq))
    cos_s, sin_s, _, _ = _rope_tables(past + jnp.arange(n_new))
    cos_s, sin_s = jnp.tile(cos_s, (dec_batch, 1)), jnp.tile(sin_s, (dec_batch, 1))
    mlp_wu, mlp_wd = mlp_w_up.astype(BF16), mlp_w_down.astype(BF16)
    sgu_win, sgu_wout = sgu_w_in.astype(BF16), sgu_w_out.astype(BF16)
    mla_wo = mla_w_o.astype(BF16)
    reps = SGU_CHUNK // n_new
    sgu_ws_s = jnp.tile(sgu_w_s[:, :, :n_new, :n_new], (1, 1, reps, reps))
    sgu_b_p = sgu_b_s[..., None]
    sgu_b_smp = jnp.tile(sgu_b_s[:, :, :n_new], (1, 1, reps))[..., None]

    n_mla = mla_w_in.shape[0]
    cache_p = cache_s = None
    sgu_v_s = []
    for i in range(depth):
        j = i // 2
        g_mix = mix_norm_g[i].reshape(1, D_MODEL)
        g_mlp = mlp_norm_g[i].reshape(1, D_MODEL)
        if i % 2 == 0:
            ng = sgu_norm_g[j].reshape(1, D_SGU)
            xp = _sgu_call(xp, g_mix, sgu_win, ng, sgu_w_s, sgu_b_p, sgu_wout, j,
                           block_diag=False, emit_v=False)
            xs, v_new = _sgu_call(xs, g_mix, sgu_win, ng, sgu_ws_s, sgu_b_smp, sgu_wout, j,
                                  block_diag=True, emit_v=True)
            sgu_v_s.append(v_new.reshape(dec_batch, n_new, D_SGU))
            xp = _mlp_call(xp, g_mlp, mlp_wu, mlp_wd, i)
            xs = _mlp_call(xs, g_mlp, mlp_wu, mlp_wd, i)
        else:
            win, wuq, wuq_t, wk, wk_t, wv, kg, gains = _prep_mla(
                mla_w_in[j], mla_w_uq[j], mla_w_ukv[j], mla_q_nope_g[j], mla_q_rope_g[j],
                mla_k_nope_g[j], mla_k_rope_g[j])
            qg, kvg = mla_q_norm_g[j].reshape(1, -1), mla_kv_norm_g[j].reshape(1, -1)
            *cache_p, q_t, k, vt = _mla_proj_call(
                xp, g_mix, win, qg, kvg, _key_tables(cos_p, sin_p, gains), wuq_t,
                _query_tables_t(cos_pt, sin_pt, gains, min(ATTN_BLOCK, seq)), j, n_mla, cache_p,
                kv_weights=(wk, wv.T, kg))
            attn = _prompt_attn_call(q_t, k, vt, batch, seq)
            xp = _mlp_call(xp, g_mlp, mlp_wu, mlp_wd, i, attn=attn, wo=mla_wo, wo_layer=j)
            *cache_s, krpad, q = _mla_proj_call(
                xs, g_mix, win, qg, kvg, _key_tables(cos_s, sin_s, gains), wuq,
                _query_tables(cos_s, sin_s, gains), j, n_mla, cache_s)
            kn, vn = _kv_expand_call(cache_s[0], j, krpad, wk, wv, kg)
            attn = _sample_attn_call(q, cache_kv_latent, cache_k_rope, j, wk_t, wv, gains["k_nope_col"],
                                     kn, vn, dec_batch, n_new, past)
            xs = _mlp_call(xs, g_mlp, mlp_wu, mlp_wd, i, attn=attn, wo=mla_wo, wo_layer=j)
    return (xp.reshape(batch, seq, D_MODEL), xs.reshape(dec_batch, n_new, D_MODEL),
            cache_p[0].reshape(n_mla, batch, seq, KV_LORA), cache_p[1].reshape(n_mla, batch, seq, ROPE_DIM),
            cache_s[0].reshape(n_mla, dec_batch, n_new, KV_LORA),
            cache_s[1].reshape(n_mla, dec_batch, n_new, ROPE_DIM), jnp.stack(sgu_v_s))
```
